```python
import jax, jax.numpy as jnp
from jax import lax
import numpy as np

D_MODEL = 1024
BATCH = 16
SEQ = 4096
DEPTH = 4

CHUNK = 64
QBLK = 128
EPS = 1e-6

A_HEADS = 8
A_HEAD_DIM = 64
A_WIDTH = A_HEADS * A_HEAD_DIM
A_DECAY_LORA = 64
A_AAA_LORA = 64
A_GATE_LORA = 128
A_GN_EPS = 64e-5
A_DECAY_SCALE = 0.6065306597126334
A_SHIFT_COLS = 3 * A_WIDTH + A_DECAY_LORA + A_AAA_LORA + A_GATE_LORA
A_SPLIT_POINTS = (A_WIDTH, 2 * A_WIDTH, 3 * A_WIDTH, 3 * A_WIDTH + A_DECAY_LORA,
                  3 * A_WIDTH + A_DECAY_LORA + A_AAA_LORA)

B_HEADS = 8
B_HEAD_DIM = 64
B_WIDTH = B_HEADS * B_HEAD_DIM

C_HEADS = 4
C_QK_DIM = 64
C_V_DIM = 2 * C_QK_DIM
C_QK_COLS = C_HEADS * 2 * C_QK_DIM
C_WIDTH = C_HEADS * C_V_DIM
ROPE_THETA = 500000.0
C_ROT_DIMS = C_QK_DIM // 4

N_BRANCHES = 3
D_FF = 2816
CONV_WIDTH = 3

GROUP_POINTS = (A_SHIFT_COLS,
                A_SHIFT_COLS + 3 * B_WIDTH,
                A_SHIFT_COLS + 3 * B_WIDTH + 2 * C_QK_COLS + C_WIDTH)
IN_COLS = GROUP_POINTS[2] + N_BRANCHES * D_MODEL

kernel_name = "hybrid_rwkv7_stickbreak_diffattn_convffn"


def rms_norm(x, g):
    xf = x.astype(jnp.float32)
    y = xf * lax.rsqrt(jnp.mean(xf * xf, axis=-1, keepdims=True) + EPS)
    return (y * g.astype(jnp.float32)).astype(x.dtype)


def causal_shift(z, n):
    return jnp.pad(z, ((0, 0), (n, 0), (0, 0)))[:, :z.shape[1]]


def sweep_query_blocks(block_fn, q_list, kv_list):
    t_len = q_list[0].shape[2]
    outs = []
    for lo in range(0, t_len, QBLK):
        hi = lo + QBLK
        outs.append(block_fn(lo, *[q[:, :, lo:hi] for q in q_list],
                             *[z[:, :, :hi] for z in kv_list]))
    return jnp.concatenate(outs, axis=2)


def partial_rope(x, cos, sin):
    half = C_ROT_DIMS // 2
    x1, x2, xp = x[..., :half], x[..., half:C_ROT_DIMS], x[..., C_ROT_DIMS:]
    return jnp.concatenate([x1 * cos - x2 * sin, x1 * sin + x2 * cos, xp], axis=-1)


def rwkv7_time_mix(pa, mu, w0, w_up, a0, a_up, g_up, k_k, k_a, r_k, ln_w, ln_b):
    f32 = jnp.float32
    bsz, t_len, _ = pa.shape
    pa = pa + (causal_shift(pa, 1) - pa) * mu
    r, k, v, dw, da, dg = jnp.split(pa, A_SPLIT_POINTS, axis=-1)
    decay = jnp.exp(-A_DECAY_SCALE * jax.nn.sigmoid((w0 + jnp.tanh(dw) @ w_up).astype(f32)))
    a = jax.nn.sigmoid(a0 + da @ a_up)
    g = jax.nn.sigmoid(dg) @ g_up
    kk = k * k_k
    k = k * (1 + (a - 1) * k_a)
    heads = lambda z: z.reshape(bsz, t_len, A_HEADS, A_HEAD_DIM).astype(f32)
    r, k, v, a, kk, decay = heads(r), heads(k), heads(v), heads(a), heads(kk), heads(decay)
    kk = kk * lax.rsqrt(jnp.maximum(jnp.sum(kk * kk, axis=-1, keepdims=True), 1e-24))

    def step(S, inp):
        r_t, w_t, k_t, v_t, kk_t, a_t = inp
        sa = jnp.einsum('bhvk,bhk->bhv', S, -kk_t)
        S = (S * w_t[:, :, None, :] + sa[..., None] * (kk_t * a_t)[:, :, None, :]
             + v_t[..., None] * k_t[:, :, None, :])
        return S, jnp.einsum('bhvk,bhk->bhv', S, r_t)

    xs = tuple(jnp.swapaxes(z, 0, 1) for z in (r, decay, k, v, kk, a))
    S0 = jnp.zeros((bsz, A_HEADS, A_HEAD_DIM, A_HEAD_DIM), f32)
    _, o = lax.scan(step, S0, xs)
    o = jnp.swapaxes(o, 0, 1)
    mean = jnp.mean(o, axis=-1, keepdims=True)
    var = jnp.mean(jnp.square(o - mean), axis=-1, keepdims=True)
    o = ((o - mean) * lax.rsqrt(var + A_GN_EPS)).reshape(bsz, t_len, A_WIDTH)
    o = o * ln_w.astype(f32) + ln_b.astype(f32)
    bonus = jnp.sum(r * k * r_k.astype(f32), axis=-1, keepdims=True) * v
    o = o + bonus.reshape(bsz, t_len, A_WIDTH)
    return o.astype(pa.dtype) * g


def stick_breaking_attention(pb):
    f32 = jnp.float32
    bsz, t_len, _ = pb.shape
    q, k, v = (z.reshape(bsz, t_len, B_HEADS, B_HEAD_DIM).transpose(0, 2, 1, 3)
               for z in jnp.split(pb, 3, axis=-1))
    scale = B_HEAD_DIM ** -0.5

    def block(lo, qb, kb, vb):
        z = jnp.einsum('bhqd,bhkd->bhqk', qb, kb).astype(f32) * scale
        t = lo + jnp.arange(qb.shape[2])[:, None]
        s = jnp.arange(kb.shape[2])[None, :]
        mask = s < t
        log_skip = jnp.where(mask, jax.nn.log_sigmoid(-z), 0.0)
        later = lax.cumsum(log_skip, axis=3, reverse=True) - log_skip
        A = jnp.where(mask, jnp.exp(jax.nn.log_sigmoid(z) + later), 0.0)
        return jnp.einsum('bhqk,bhkd->bhqd', A.astype(vb.dtype), vb)

    o = sweep_query_blocks(block, (q,), (k, v))
    return o.transpose(0, 2, 1, 3).reshape(bsz, t_len, B_WIDTH)


def differential_attention(pc, lam_vecs, subln_g, lambda_init, cos, sin):
    f32 = jnp.float32
    bsz, t_len, _ = pc.shape
    q, k, v = jnp.split(pc, (C_QK_COLS, 2 * C_QK_COLS), axis=-1)
    qk_heads = lambda z: z.reshape(bsz, t_len, C_HEADS, 2, C_QK_DIM).transpose(3, 0, 2, 1, 4)
    q = partial_rope(qk_heads(q), cos, sin)
    k = partial_rope(qk_heads(k), cos, sin)
    v = v.reshape(bsz, t_len, C_HEADS, C_V_DIM).transpose(0, 2, 1, 3)
    lv = lam_vecs.astype(f32)
    lam = jnp.exp(jnp.sum(lv[0] * lv[1])) - jnp.exp(jnp.sum(lv[2] * lv[3])) + lambda_init
    scale = C_QK_DIM ** -0.5

    def block(lo, q1b, q2b, k1b, k2b, vb):
        q_chunk = (lo + jnp.arange(q1b.shape[2])) // CHUNK
        k_chunk = jnp.arange(k1b.shape[2]) // CHUNK
        mask = k_chunk[None, :] <= q_chunk[:, None]

        def attn_map(qb, kb):
            z = jnp.einsum('bhqd,bhkd->bhqk', qb, kb).astype(f32) * scale
            return jax.nn.softmax(jnp.where(mask, z, -jnp.inf), axis=-1)

        A = attn_map(q1b, k1b) - lam * attn_map(q2b, k2b)
        return jnp.einsum('bhqk,bhkd->bhqd', A.astype(vb.dtype), vb)

    o = sweep_query_blocks(block, (q[0], q[1]), (k[0], k[1], v))
    o = rms_norm(o, subln_g) * (1.0 - lambda_init)
    return o.transpose(0, 2, 1, 3).reshape(bsz, t_len, C_WIDTH)


def conv_ffn(h, w_up, conv_w, conv_b, w_down):
    u = h @ w_up
    u = conv_b + sum(conv_w[j] * causal_shift(u, CONV_WIDTH - 1 - j) for j in range(CONV_WIDTH))
    gate, val = jnp.split(u, 2, axis=-1)
    return (jax.nn.silu(gate) * val) @ w_down


def setup_inputs(seed: int = 0) -> dict:
    key = jax.random.key(seed)
    ks = iter(jax.random.split(key, 40))
    nrm = lambda shape, scale: scale * jax.random.normal(next(ks), shape, jnp.float32)
    L, D = DEPTH, D_MODEL
    return {
        'x': nrm((BATCH, SEQ, D), 1.0),
        'norm_mix_g': 1.0 + nrm((L, D), 0.02),
        'norm_ffn_g': 1.0 + nrm((L, D), 0.02),
        'w_in': nrm((L, D, IN_COLS), D ** -0.5),
        'rwkv_mu': jax.random.uniform(next(ks), (L, A_SHIFT_COLS), jnp.float32),
        'rwkv_w0': nrm((L, A_WIDTH), 0.5),
        'rwkv_w_up': nrm((L, A_DECAY_LORA, A_WIDTH), 0.5 * A_DECAY_LORA ** -0.5),
        'rwkv_a0': nrm((L, A_WIDTH), 0.5),
        'rwkv_a_up': nrm((L, A_AAA_LORA, A_WIDTH), 0.5 * A_AAA_LORA ** -0.5),
        'rwkv_g_up': nrm((L, A_GATE_LORA, A_WIDTH), A_GATE_LORA ** -0.5),
        'rwkv_k_k': 0.85 + nrm((L, A_WIDTH), 0.05),
        'rwkv_k_a': 1.0 + nrm((L, A_WIDTH), 0.05),
        'rwkv_r_k': nrm((L, A_HEADS, A_HEAD_DIM), 0.1),
        'rwkv_ln_w': 1.0 + nrm((L, A_WIDTH), 0.02),
        'rwkv_ln_b': nrm((L, A_WIDTH), 0.02),
        'diff_lambda': nrm((L, 4, C_QK_DIM), 0.1),
        'diff_subln_g': 1.0 + nrm((L, C_V_DIM), 0.02),
        'w_branch_a': nrm((L, A_WIDTH, D), A_WIDTH ** -0.5),
        'w_branch_b': nrm((L, B_WIDTH, D), B_WIDTH ** -0.5),
        'w_branch_c': nrm((L, C_WIDTH, D), C_WIDTH ** -0.5),
        'w_out': nrm((L, D, D), D ** -0.5),
        'ffn_w_up': nrm((L, D, 2 * D_FF), D ** -0.5),
        'ffn_conv_w': nrm((L, CONV_WIDTH, 2 * D_FF), CONV_WIDTH ** -0.5),
        'ffn_conv_b': nrm((L, 2 * D_FF), 0.02),
        'ffn_w_down': nrm((L, D_FF, D), D_FF ** -0.5),
        'norm_final_g': 1.0 + nrm((D,), 0.02),
    }


def reference(x, norm_mix_g, norm_ffn_g, w_in, rwkv_mu, rwkv_w0, rwkv_w_up, rwkv_a0,
              rwkv_a_up, rwkv_g_up, rwkv_k_k, rwkv_k_a, rwkv_r_k, rwkv_ln_w, rwkv_ln_b,
              diff_lambda, diff_subln_g, w_branch_a, w_branch_b, w_branch_c, w_out,
              ffn_w_up, ffn_conv_w, ffn_conv_b, ffn_w_down, norm_final_g):
    t_len = x.shape[1]
    inv_freq = ROPE_THETA ** (-jnp.arange(0, C_ROT_DIMS, 2, dtype=jnp.float32) / C_ROT_DIMS)
    ang = jnp.arange(t_len, dtype=jnp.float32)[:, None] * inv_freq[None, :]
    cos, sin = jnp.cos(ang).astype(x.dtype), jnp.sin(ang).astype(x.dtype)

    for l in range(DEPTH):
        h = rms_norm(x, norm_mix_g[l])
        proj = h @ w_in[l]
        pa, pb, pc, pg = jnp.split(proj, GROUP_POINTS, axis=-1)
        oa = rwkv7_time_mix(pa, rwkv_mu[l], rwkv_w0[l], rwkv_w_up[l], rwkv_a0[l], rwkv_a_up[l],
                            rwkv_g_up[l], rwkv_k_k[l], rwkv_k_a[l], rwkv_r_k[l],
                            rwkv_ln_w[l], rwkv_ln_b[l])
        ob = stick_breaking_attention(pb)
        lambda_init = 0.8 - 0.6 * float(np.exp(-0.3 * l))
        oc = differential_attention(pc, diff_lambda[l], diff_subln_g[l], lambda_init, cos, sin)
        ga, gb, gc = jnp.split(jax.nn.sigmoid(pg), N_BRANCHES, axis=-1)
        merged = ga * (oa @ w_branch_a[l]) + gb * (ob @ w_branch_b[l]) + gc * (oc @ w_branch_c[l])
        x = x + merged @ w_out[l]
        x = x + conv_ffn(rms_norm(x, norm_ffn_g[l]), ffn_w_up[l], ffn_conv_w[l],
                         ffn_conv_b[l], ffn_w_down[l])
    return rms_norm(x, norm_final_g)
```

```python
import functools

import numpy as np
import jax
import jax.numpy as jnp
from jax import lax
from jax.experimental import pallas as pl
from jax.experimental.pallas import tpu as pltpu

F32 = jnp.float32
BF16 = jnp.bfloat16
HIGHEST = lax.Precision.HIGHEST

D_MODEL = 1024
DEPTH = 4
CHUNK = 64
EPS = 1e-6

A_HEADS = 8
A_HEAD_DIM = 64
A_WIDTH = A_HEADS * A_HEAD_DIM
A_DECAY_LORA = 64
A_AAA_LORA = 64
A_GATE_LORA = 128
A_GN_EPS = 64e-5
A_DECAY_SCALE = 0.6065306597126334
A_SHIFT_COLS = 3 * A_WIDTH + A_DECAY_LORA + A_AAA_LORA + A_GATE_LORA

B_HEADS = 8
B_HEAD_DIM = 64
B_WIDTH = B_HEADS * B_HEAD_DIM

C_HEADS = 4
C_QK_DIM = 64
C_V_DIM = 2 * C_QK_DIM
C_QK_COLS = C_HEADS * 2 * C_QK_DIM
C_WIDTH = C_HEADS * C_V_DIM
ROPE_THETA = 500000.0
C_ROT_DIMS = C_QK_DIM // 4

D_FF = 2816
CONV_WIDTH = 3

LANES = 128
SUBLANES = 8

PA_OFF = 0
PA_PAD = 2048
PB_OFF = PA_PAD
PC_OFF = PB_OFF + 3 * B_WIDTH
PG_OFF = PC_OFF + 2 * C_QK_COLS + C_WIDTH
PROJ_COLS = PG_OFF + 3 * D_MODEL

VMEM_LIMIT = 56 * 1024 * 1024

EXP_ZERO_BELOW = -104.0


def _cparams(sem):
    return pltpu.CompilerParams(dimension_semantics=sem, vmem_limit_bytes=VMEM_LIMIT)


def _dot(a, b):
    return jnp.dot(a, b, preferred_element_type=F32)


def _dot_nt(a, b):
    return lax.dot_general(a, b, (((1,), (1,)), ((), ())), preferred_element_type=F32)


def _dot_hi(a, b):
    return jnp.dot(a, b, preferred_element_type=F32, precision=HIGHEST)


def _dot_nt_hi(a, b):
    return lax.dot_general(a, b, (((1,), (1,)), ((), ())), preferred_element_type=F32,
                           precision=HIGHEST)


def _inproj_kernel(x_ref, g_ref, w_ref, o_ref, h_ref):
    @pl.when(pl.program_id(1) == 0)
    def _():
        x = x_ref[...]
        ms = jnp.mean(x * x, axis=-1, keepdims=True)
        h_ref[...] = (x * lax.rsqrt(ms + EPS) * g_ref[...]).astype(BF16)

    o_ref[...] = _dot(h_ref[...], w_ref[...])


def in_projection(x2, g, w_pad, tm=512, tn=1024):
    n = x2.shape[0]
    return pl.pallas_call(
        _inproj_kernel,
        grid=(n // tm, PROJ_COLS // tn),
        in_specs=[
            pl.BlockSpec((tm, D_MODEL), lambda i, j: (i, 0)),
            pl.BlockSpec((1, D_MODEL), lambda i, j: (0, 0)),
            pl.BlockSpec((D_MODEL, tn), lambda i, j: (0, j)),
        ],
        out_specs=pl.BlockSpec((tm, tn), lambda i, j: (i, j)),
        out_shape=jax.ShapeDtypeStruct((n, PROJ_COLS), F32),
        scratch_shapes=[pltpu.VMEM((tm, D_MODEL), BF16)],
        compiler_params=_cparams(("parallel", "arbitrary")),
        name="in_projection",
    )(x2, g, w_pad)


def _rwkv_prep_kernel(pa_ref, prev_ref, mu_ref, w0_ref, wup_ref, a0_ref, aup_ref, gup_ref,
                      r_ref, k_ref, v_ref, lw_ref, a_ref, g_ref):
    tm = pa_ref.shape[0]
    cur = pa_ref[:, :A_SHIFT_COLS]
    prev_last = prev_ref[SUBLANES - 1:SUBLANES, :A_SHIFT_COLS]
    prev_last = jnp.where(pl.program_id(1) == 0, 0.0, prev_last)
    shifted = pltpu.roll(cur, 1, axis=0)
    row = lax.broadcasted_iota(jnp.int32, (tm, 1), 0)
    shifted = jnp.where(row == 0, prev_last, shifted)
    xs = cur + (shifted - cur) * mu_ref[...]

    w = A_WIDTH
    r = xs[:, 0:w]
    k = xs[:, w:2 * w]
    v = xs[:, 2 * w:3 * w]
    dw = xs[:, 3 * w:3 * w + A_DECAY_LORA]
    da = xs[:, 3 * w + A_DECAY_LORA:3 * w + A_DECAY_LORA + A_AAA_LORA]
    dg = xs[:, 3 * w + A_DECAY_LORA + A_AAA_LORA:A_SHIFT_COLS]

    lw = -A_DECAY_SCALE * jax.nn.sigmoid(w0_ref[...] + _dot(jnp.tanh(dw).astype(BF16), wup_ref[...]))
    a = jax.nn.sigmoid(a0_ref[...] + _dot(da.astype(BF16), aup_ref[...]))
    g = _dot(jax.nn.sigmoid(dg).astype(BF16), gup_ref[...])

    for h in range(A_HEADS):
        sl = slice(h * A_HEAD_DIM, (h + 1) * A_HEAD_DIM)
        r_ref[0, h] = r[:, sl]
        k_ref[0, h] = k[:, sl]
        v_ref[0, h] = v[:, sl]
        lw_ref[0, h] = lw[:, sl]
        a_ref[0, h] = a[:, sl]
        g_ref[0, h] = g[:, sl]


def rwkv_prep(proj, bsz, t_len, mu, w0, w_up, a0, a_up, g_up, tm=256):
    tpb = t_len // tm
    rb = tm // SUBLANES

    def prev_map(b, i):
        return (jnp.maximum((b * tpb + i) * rb - 1, 0), 0)

    head_shape = jax.ShapeDtypeStruct((bsz, A_HEADS, t_len, A_HEAD_DIM), F32)
    head_spec = pl.BlockSpec((1, A_HEADS, tm, A_HEAD_DIM), lambda b, i: (b, 0, i, 0))
    full = lambda shape: pl.BlockSpec(shape, lambda b, i: (0,) * len(shape))
    return pl.pallas_call(
        _rwkv_prep_kernel,
        grid=(bsz, tpb),
        in_specs=[
            pl.BlockSpec((tm, PA_PAD), lambda b, i: (b * tpb + i, 0)),
            pl.BlockSpec((SUBLANES, PA_PAD), prev_map),
            full((1, A_SHIFT_COLS)),
            full((1, A_WIDTH)),
            full((A_DECAY_LORA, A_WIDTH)),
            full((1, A_WIDTH)),
            full((A_AAA_LORA, A_WIDTH)),
            full((A_GATE_LORA, A_WIDTH)),
        ],
        out_specs=[head_spec] * 6,
        out_shape=[head_shape] * 6,
        compiler_params=_cparams(("parallel", "arbitrary")),
        name="rwkv_prep",
    )(proj, proj, mu, w0, w_up, a0, a_up, g_up)


def _rwkv_scan_kernel(r_ref, k_ref, v_ref, lw_ref, a_ref, g_ref, kk_ref, ka_ref, rk_ref, lnw_ref,
                      lnb_ref, o_ref, s_ref, *, chunk):
    c = chunk
    d = A_HEAD_DIM

    @pl.when(pl.program_id(1) == 0)
    def _():
        s_ref[...] = jnp.zeros_like(s_ref)

    ti = lax.broadcasted_iota(jnp.int32, (c, c), 0)
    si = lax.broadcasted_iota(jnp.int32, (c, c), 1)
    strict = si < ti
    incl = si <= ti
    tri_incl = incl.astype(F32)
    n_double = int(np.ceil(np.log2(c)))

    outs = []
    for h in range(A_HEADS):
        r = r_ref[0, h]
        k = k_ref[0, h]
        v = v_ref[0, h]
        lw = lw_ref[0, h]
        a = a_ref[0, h]
        kk = k * kk_ref[h]
        kk = kk * lax.rsqrt(jnp.maximum(jnp.sum(kk * kk, axis=-1, keepdims=True), 1e-24))
        kmod = k * (1.0 + (a - 1.0) * ka_ref[h])
        al = -kk
        be = kk * a

        cs = _dot_hi(tri_incl, lw)
        gam = jnp.exp(cs)
        gprev = jnp.exp(cs - lw)
        ginv = jnp.exp(-cs)
        gend = gam[c - 1:c, :]
        at = al * gprev
        bt = be * ginv
        kt = kmod * ginv
        rt = r * gam

        lhs = jnp.concatenate([at, rt], axis=0)
        fb = _dot_nt_hi(lhs, bt)
        fk = _dot_nt_hi(lhs, kt)
        m_ab = jnp.where(strict, fb[:c], 0.0)
        p_rb = jnp.where(incl, fb[c:], 0.0)
        m_ak = jnp.where(strict, fk[:c], 0.0)
        p_rk = jnp.where(incl, fk[c:], 0.0)

        x = jnp.concatenate([at, _dot_hi(m_ak, v)], axis=1)
        p = m_ab
        for j in range(n_double):
            x = x + _dot_hi(p, x)
            if j + 1 < n_double:
                p = _dot_hi(p, p)

        s0 = s_ref[h]
        y = _dot_hi(p_rb, x)
        o = (_dot_nt_hi(y[:, :d] + rt, s0) + y[:, d:] + _dot_hi(p_rk, v))

        xb = _dot_hi(x.T, bt * gend)
        vk = _dot_hi(v.T, kt * gend)
        s_ref[h] = s0 * gend + _dot_hi(s0, xb[:d]) + xb[d:] + vk

        mean = jnp.mean(o, axis=-1, keepdims=True)
        var = jnp.mean(jnp.square(o - mean), axis=-1, keepdims=True)
        o = (o - mean) * lax.rsqrt(var + A_GN_EPS)
        o = o * lnw_ref[h] + lnb_ref[h]
        bonus = jnp.sum(r * kmod * rk_ref[h], axis=-1, keepdims=True) * v
        outs.append((o + bonus) * g_ref[0, h])

    o_ref[0] = jnp.concatenate(outs, axis=1)


def rwkv_scan(r, k, v, lw, a, g, k_k, k_a, r_k, ln_w, ln_b, chunk=64):
    bsz, _, t_len, _ = r.shape
    head_spec = pl.BlockSpec((1, A_HEADS, chunk, A_HEAD_DIM), lambda b, i: (b, 0, i, 0))
    par_spec = pl.BlockSpec((A_HEADS, 1, A_HEAD_DIM), lambda b, i: (0, 0, 0))
    per_head = lambda z: z.reshape(A_HEADS, 1, A_HEAD_DIM)
    return pl.pallas_call(
        functools.partial(_rwkv_scan_kernel, chunk=chunk),
        grid=(bsz, t_len // chunk),
        in_specs=[head_spec] * 6 + [par_spec] * 5,
        out_specs=pl.BlockSpec((1, chunk, A_WIDTH), lambda b, i: (b, i, 0)),
        out_shape=jax.ShapeDtypeStruct((bsz, t_len, A_WIDTH), F32),
        scratch_shapes=[pltpu.VMEM((A_HEADS, A_HEAD_DIM, A_HEAD_DIM), F32)],
        compiler_params=_cparams(("parallel", "arbitrary")),
        name="rwkv_scan",
    )(r, k, v, lw, a, g, per_head(k_k), per_head(k_a), per_head(r_k), per_head(ln_w), per_head(ln_b))


def _sb_attn_kernel(q_ref, k_ref, v_ref, o_ref, *, blk):
    i = pl.program_id(2)
    scale = B_HEAD_DIM ** -0.5
    q = q_ref[...] * scale
    lane = lax.broadcasted_iota(jnp.int32, (1, LANES), 1)
    head0 = lane < B_HEAD_DIM
    qh = (jnp.where(head0, q, 0.0).astype(BF16), jnp.where(head0, 0.0, q).astype(BF16))

    ju = lax.broadcasted_iota(jnp.int32, (blk, blk + LANES), 0)
    su = lax.broadcasted_iota(jnp.int32, (blk, blk + LANES), 1)
    u = jnp.where((ju > su) | (su >= blk), 1.0, 0.0).astype(BF16)
    u2 = jnp.concatenate([u, u], axis=0)

    t_idx = i * blk + lax.broadcasted_iota(jnp.int32, (blk, blk), 0)
    s_loc = lax.broadcasted_iota(jnp.int32, (blk, blk), 1)

    def cond(carry):
        kb, cmax = carry[0], carry[1]
        return jnp.logical_and(kb >= 0, cmax > EXP_ZERO_BELOW)

    def body(carry):
        kb, _, c0, c1, acc0, acc1 = carry
        start = pl.multiple_of(kb * blk, blk)
        kblk = k_ref[pl.ds(start, blk), :].astype(BF16)
        vblk = v_ref[pl.ds(start, blk), :].astype(BF16)
        mask = (kb * blk + s_loc) < t_idx
        new = []
        for h, (c, acc) in enumerate(((c0, acc0), (c1, acc1))):
            z = _dot_nt(qh[h], kblk)
            sp = jnp.maximum(z, 0.0) + jnp.log1p(jnp.exp(-jnp.abs(z)))
            ls = jnp.where(mask, -sp, 0.0)
            ls_hi = ls.astype(BF16)
            ls_lo = (ls - ls_hi.astype(F32)).astype(BF16)
            lu = _dot(jnp.concatenate([ls_hi, ls_lo], axis=1), u2)
            later = lu[:, :blk]
            rowsum = lu[:, blk:]
            amat = jnp.where(mask, jnp.exp((z - sp) + later + c), 0.0)
            new.append((c + rowsum, acc + _dot(amat.astype(BF16), vblk)))
        (c0, acc0), (c1, acc1) = new
        cmax = jnp.maximum(jnp.max(c0), jnp.max(c1))
        return kb - 1, cmax, c0, c1, acc0, acc1

    zeros = jnp.zeros((blk, LANES), F32)
    init = (i, jnp.float32(0.0), zeros, zeros, zeros, zeros)
    _, _, _, _, acc0, acc1 = lax.while_loop(cond, body, init)
    o_ref[...] = jnp.where(head0, acc0, acc1)


def stick_breaking_attention(proj, bsz, t_len, blk=128):
    assert blk == LANES
    nq = t_len // blk
    pairs = B_WIDTH // LANES
    qc, kc, vc = (PB_OFF // LANES, (PB_OFF + B_WIDTH) // LANES, (PB_OFF + 2 * B_WIDTH) // LANES)
    return pl.pallas_call(
        functools.partial(_sb_attn_kernel, blk=blk),
        grid=(bsz, pairs, nq),
        in_specs=[
            pl.BlockSpec((blk, LANES), lambda b, j, i: (b * nq + i, qc + j)),
            pl.BlockSpec((t_len, LANES), lambda b, j, i: (b, kc + j)),
            pl.BlockSpec((t_len, LANES), lambda b, j, i: (b, vc + j)),
        ],
        out_specs=pl.BlockSpec((blk, LANES), lambda b, j, i: (b * nq + i, j)),
        out_shape=jax.ShapeDtypeStruct((bsz * t_len, B_WIDTH), F32),
        compiler_params=_cparams(("parallel", "parallel", "arbitrary")),
        name="stick_breaking_attention",
    )(proj, proj, proj)


def _rope_kernel(q_ref, k_ref, cos_ref, sinm_ref, sinp_ref, qo_ref, ko_ref):
    cos, sinm, sinp = cos_ref[...], sinm_ref[...], sinp_ref[...]
    half = C_ROT_DIMS // 2
    scale = C_QK_DIM ** -0.5

    def rot(x):
        return x * cos + pltpu.roll(x, LANES - half, axis=1) * sinm + pltpu.roll(x, half, axis=1) * sinp

    for h in range(C_HEADS):
        sl = slice(h * LANES, (h + 1) * LANES)
        qo_ref[:, sl] = (rot(q_ref[:, sl]) * scale).astype(BF16)
        ko_ref[:, sl] = rot(k_ref[:, sl]).astype(BF16)


def rope_qk(proj, bsz, t_len, cos_t, sinm_t, sinp_t, tm=512):
    n = bsz * t_len
    tpb = t_len // tm
    qc, kc = PC_OFF // C_QK_COLS, (PC_OFF + C_QK_COLS) // C_QK_COLS
    tab = pl.BlockSpec((tm, LANES), lambda i: (i % tpb, 0))
    out = jax.ShapeDtypeStruct((n, C_QK_COLS), BF16)
    return pl.pallas_call(
        _rope_kernel,
        grid=(n // tm,),
        in_specs=[
            pl.BlockSpec((tm, C_QK_COLS), lambda i: (i, qc)),
            pl.BlockSpec((tm, C_QK_COLS), lambda i: (i, kc)),
            tab, tab, tab,
        ],
        out_specs=[pl.BlockSpec((tm, C_QK_COLS), lambda i: (i, 0))] * 2,
        out_shape=[out, out],
        compiler_params=_cparams(("parallel",)),
        name="rope_qk",
    )(proj, proj, cos_t, sinm_t, sinp_t)


def _diff_attn_kernel(q_ref, k_ref, v_ref, lam_ref, g_ref, o_ref, *, blk, lambda_init):
    i = pl.program_id(2)
    lane = lax.broadcasted_iota(jnp.int32, (1, LANES), 1)
    half0 = lane < C_QK_DIM
    q = q_ref[...]
    zero = jnp.zeros_like(q)
    qh = (jnp.where(half0, q, zero), jnp.where(half0, zero, q))

    def scores(kb):
        start = pl.multiple_of(kb * blk, blk)
        kblk = k_ref[pl.ds(start, blk), :]
        vblk = v_ref[pl.ds(start, blk), :].astype(BF16)
        return [_dot_nt(qh[s], kblk) for s in range(2)], vblk

    def update(state, z, vblk):
        m, l, acc = state
        m_new = jnp.maximum(m, jnp.max(z, axis=-1, keepdims=True))
        alpha = jnp.exp(m - m_new)
        p = jnp.exp(z - m_new)
        l = alpha * l + jnp.sum(p, axis=-1, keepdims=True)
        acc = alpha * acc + _dot(p.astype(BF16), vblk)
        return m_new, l, acc

    def body(kb, states):
        zs, vblk = scores(kb)
        return tuple(update(states[s], zs[s], vblk) for s in range(2))

    init_one = (jnp.full((blk, 1), -1e30, F32), jnp.zeros((blk, 1), F32), jnp.zeros((blk, LANES), F32))
    states = lax.fori_loop(0, i, body, (init_one, init_one))

    zs, vblk = scores(i)
    qc = lax.broadcasted_iota(jnp.int32, (blk, blk), 0) // CHUNK
    kc = lax.broadcasted_iota(jnp.int32, (blk, blk), 1) // CHUNK
    mask = kc <= qc
    states = tuple(update(states[s], jnp.where(mask, zs[s], -jnp.inf), vblk) for s in range(2))

    lv = lam_ref[...]
    lam = (jnp.exp(jnp.sum(lv[0:1] * lv[1:2], axis=-1, keepdims=True))
           - jnp.exp(jnp.sum(lv[2:3] * lv[3:4], axis=-1, keepdims=True)) + lambda_init)
    (_, l1, acc1), (_, l2, acc2) = states
    o = acc1 / l1 - lam * (acc2 / l2)
    o = o * lax.rsqrt(jnp.mean(o * o, axis=-1, keepdims=True) + EPS) * g_ref[...]
    o_ref[...] = o * (1.0 - lambda_init)


def differential_attention(q_rot, k_rot, proj, bsz, t_len, lam_vecs, subln_g, lambda_init, blk=256):
    nq = t_len // blk
    vc = (PC_OFF + 2 * C_QK_COLS) // LANES
    return pl.pallas_call(
        functools.partial(_diff_attn_kernel, blk=blk, lambda_init=lambda_init),
        grid=(bsz, C_HEADS, nq),
        in_specs=[
            pl.BlockSpec((blk, LANES), lambda b, h, i: (b * nq + i, h)),
            pl.BlockSpec((t_len, LANES), lambda b, h, i: (b, h)),
            pl.BlockSpec((t_len, LANES), lambda b, h, i: (b, vc + h)),
            pl.BlockSpec((4, C_QK_DIM), lambda b, h, i: (0, 0)),
            pl.BlockSpec((1, C_V_DIM), lambda b, h, i: (0, 0)),
        ],
        out_specs=pl.BlockSpec((blk, LANES), lambda b, h, i: (b * nq + i, h)),
        out_shape=jax.ShapeDtypeStruct((bsz * t_len, C_WIDTH), F32),
        compiler_params=_cparams(("parallel", "parallel", "arbitrary")),
        name="differential_attention",
    )(q_rot, k_rot, proj, lam_vecs, subln_g)


def _merge_kernel(x_ref, oa_ref, ob_ref, oc_ref, ga_ref, gb_ref, gc_ref, wa_ref, wb_ref, wc_ref,
                  wo_ref, o_ref):
    def branch(o_r, g_r, w_r):
        return jax.nn.sigmoid(g_r[...]) * _dot(o_r[...].astype(BF16), w_r[...])

    merged = (branch(oa_ref, ga_ref, wa_ref) + branch(ob_ref, gb_ref, wb_ref)
              + branch(oc_ref, gc_ref, wc_ref))
    o_ref[...] = x_ref[...] + _dot(merged.astype(BF16), wo_ref[...])


def merge_branches(x2, oa, ob, oc, proj, wa, wb, wc, wo, tm=512):
    n = x2.shape[0]
    g0 = PG_OFF // D_MODEL
    row = lambda width: pl.BlockSpec((tm, width), lambda i: (i, 0))
    gate = lambda j: pl.BlockSpec((tm, D_MODEL), lambda i: (i, g0 + j))
    wspec = lambda rows: pl.BlockSpec((rows, D_MODEL), lambda i: (0, 0))
    return pl.pallas_call(
        _merge_kernel,
        grid=(n // tm,),
        in_specs=[row(D_MODEL), row(A_WIDTH), row(B_WIDTH), row(C_WIDTH), gate(0), gate(1), gate(2),
                  wspec(A_WIDTH), wspec(B_WIDTH), wspec(C_WIDTH), wspec(D_MODEL)],
        out_specs=row(D_MODEL),
        out_shape=jax.ShapeDtypeStruct((n, D_MODEL), F32),
        compiler_params=_cparams(("parallel",)),
        name="merge_branches",
    )(x2, oa, ob, oc, proj, proj, proj, wa, wb, wc, wo)


def _ffn_kernel(x_ref, xp_ref, g_ref, wg_ref, wv_ref, cwg_ref, cwv_ref, cbg_ref, cbv_ref, wd_ref,
                gf_ref, o_ref, h_ref, u_ref, acc_ref, *, final_norm):
    tm = x_ref.shape[0]
    tf = wg_ref.shape[1]
    halo = SUBLANES
    j = pl.program_id(2)

    def norm(x):
        return x * lax.rsqrt(jnp.mean(x * x, axis=-1, keepdims=True) + EPS)

    @pl.when(j == 0)
    def _():
        g = g_ref[...]
        h_ref[halo:, :] = (norm(x_ref[...]) * g).astype(BF16)
        hp = norm(xp_ref[...]) * g
        h_ref[:halo, :] = jnp.where(pl.program_id(1) == 0, 0.0, hp).astype(BF16)
        acc_ref[...] = jnp.zeros_like(acc_ref)

    h = h_ref[...]
    u_ref[:, :tf] = _dot(h, wg_ref[...])
    u_ref[:, tf:] = _dot(h, wv_ref[...])

    def conv(lo, cw_ref, cb_ref):
        out = cb_ref[...]
        for tap in range(CONV_WIDTH):
            rows = pl.ds(halo - (CONV_WIDTH - 1 - tap), tm)
            out = out + cw_ref[tap:tap + 1, :] * u_ref[rows, lo:lo + tf]
        return out

    gate = conv(0, cwg_ref, cbg_ref)
    val = conv(tf, cwv_ref, cbv_ref)
    act = (gate * jax.nn.sigmoid(gate) * val).astype(BF16)
    acc_ref[...] += _dot(act, wd_ref[...])

    @pl.when(j == pl.num_programs(2) - 1)
    def _():
        y = x_ref[...] + acc_ref[...]
        if final_norm:
            y = norm(y) * gf_ref[...]
        o_ref[...] = y


def conv_ffn(x2, bsz, t_len, g, w_up, conv_w, conv_b, w_down, g_final, final_norm, tm=512, tf=256):
    n = x2.shape[0]
    tpb = t_len // tm
    nf = D_FF // tf
    rb = tm // SUBLANES

    def prev_map(b, i, j):
        return (jnp.maximum((b * tpb + i) * rb - 1, 0), 0)

    return pl.pallas_call(
        functools.partial(_ffn_kernel, final_norm=final_norm),
        grid=(bsz, tpb, nf),
        in_specs=[
            pl.BlockSpec((tm, D_MODEL), lambda b, i, j: (b * tpb + i, 0)),
            pl.BlockSpec((SUBLANES, D_MODEL), prev_map),
            pl.BlockSpec((1, D_MODEL), lambda b, i, j: (0, 0)),
            pl.BlockSpec((D_MODEL, tf), lambda b, i, j: (0, j)),
            pl.BlockSpec((D_MODEL, tf), lambda b, i, j: (0, nf + j)),
            pl.BlockSpec((CONV_WIDTH, tf), lambda b, i, j: (0, j)),
            pl.BlockSpec((CONV_WIDTH, tf), lambda b, i, j: (0, nf + j)),
            pl.BlockSpec((1, tf), lambda b, i, j: (0, j)),
            pl.BlockSpec((1, tf), lambda b, i, j: (0, nf + j)),
            pl.BlockSpec((tf, D_MODEL), lambda b, i, j: (j, 0)),
            pl.BlockSpec((1, D_MODEL), lambda b, i, j: (0, 0)),
        ],
        out_specs=pl.BlockSpec((tm, D_MODEL), lambda b, i, j: (b * tpb + i, 0)),
        out_shape=jax.ShapeDtypeStruct((n, D_MODEL), F32),
        scratch_shapes=[
            pltpu.VMEM((tm + SUBLANES, D_MODEL), BF16),
            pltpu.VMEM((tm + SUBLANES, 2 * tf), F32),
            pltpu.VMEM((tm, D_MODEL), F32),
        ],
        compiler_params=_cparams(("parallel", "parallel", "arbitrary")),
        name="conv_ffn",
    )(x2, x2, g, w_up, w_up, conv_w, conv_w, conv_b, conv_b, w_down, g_final)


def _rope_tables(t_len):
    half = C_ROT_DIMS // 2
    inv_freq = ROPE_THETA ** (-jnp.arange(0, C_ROT_DIMS, 2, dtype=F32) / C_ROT_DIMS)
    ang = jnp.arange(t_len, dtype=F32)[:, None] * inv_freq[None, :]
    cos, sin = jnp.cos(ang), jnp.sin(ang)
    rest = C_QK_DIM - C_ROT_DIMS
    one = jnp.ones((t_len, rest), F32)
    zero = jnp.zeros((t_len, rest), F32)
    zh = jnp.zeros((t_len, half), F32)
    cos_t = jnp.concatenate([cos, cos, one], axis=1)
    sinm_t = jnp.concatenate([-sin, zh, zero], axis=1)
    sinp_t = jnp.concatenate([zh, sin, zero], axis=1)
    dup = lambda z: jnp.concatenate([z, z], axis=1)
    return dup(cos_t), dup(sinm_t), dup(sinp_t)


def _pad_w_in(w):
    pad = jnp.zeros((D_MODEL, PA_PAD - A_SHIFT_COLS), w.dtype)
    return jnp.concatenate([w[:, :A_SHIFT_COLS], pad, w[:, A_SHIFT_COLS:]], axis=1).astype(BF16)


def kernel(x, norm_mix_g, norm_ffn_g, w_in, rwkv_mu, rwkv_w0, rwkv_w_up, rwkv_a0, rwkv_a_up,
           rwkv_g_up, rwkv_k_k, rwkv_k_a, rwkv_r_k, rwkv_ln_w, rwkv_ln_b, diff_lambda, diff_subln_g,
           w_branch_a, w_branch_b, w_branch_c, w_out, ffn_w_up, ffn_conv_w, ffn_conv_b, ffn_w_down,
           norm_final_g):
    bsz, t_len, _ = x.shape
    n = bsz * t_len
    cos_t, sinm_t, sinp_t = _rope_tables(t_len)
    row = lambda z: z.reshape(1, -1)
    x2 = x.reshape(n, D_MODEL)

    for l in range(DEPTH):
        proj = in_projection(x2, row(norm_mix_g[l]), _pad_w_in(w_in[l]))

        r, k, v, lw, a, g = rwkv_prep(
            proj, bsz, t_len, row(rwkv_mu[l]), row(rwkv_w0[l]), rwkv_w_up[l].astype(BF16),
            row(rwkv_a0[l]), rwkv_a_up[l].astype(BF16), rwkv_g_up[l].astype(BF16))
        oa = rwkv_scan(r, k, v, lw, a, g, rwkv_k_k[l], rwkv_k_a[l], rwkv_r_k[l].reshape(-1),
                       rwkv_ln_w[l], rwkv_ln_b[l]).reshape(n, A_WIDTH)

        ob = stick_breaking_attention(proj, bsz, t_len)

        lambda_init = 0.8 - 0.6 * float(np.exp(-0.3 * l))
        q_rot, k_rot = rope_qk(proj, bsz, t_len, cos_t, sinm_t, sinp_t)
        oc = differential_attention(q_rot, k_rot, proj, bsz, t_len, diff_lambda[l],
                                    row(diff_subln_g[l]), lambda_init)

        x2 = merge_branches(x2, oa, ob, oc, proj, w_branch_a[l].astype(BF16),
                            w_branch_b[l].astype(BF16), w_branch_c[l].astype(BF16),
                            w_out[l].astype(BF16))

        x2 = conv_ffn(x2, bsz, t_len, row(norm_ffn_g[l]), ffn_w_up[l].astype(BF16), ffn_conv_w[l],
                      row(ffn_conv_b[l]), ffn_w_down[l].astype(BF16), row(norm_final_g),
                      final_norm=(l == DEPTH - 1))

    return x2.reshape(bsz, t_len, D_MODEL)
```

```python
import functools

import numpy as np
import jax
import jax.numpy as jnp
from jax import lax
from jax.experimental import pallas as pl
from jax.experimental.pallas import tpu as pltpu

F32 = jnp.float32
BF16 = jnp.bfloat16

D_MODEL = 1024
DEPTH = 4
CHUNK = 64
RWKV_CHUNK = 64
EPS = 1e-6

A_HEADS = 8
A_HEAD_DIM = 64
A_WIDTH = A_HEADS * A_HEAD_DIM
A_DECAY_LORA = 64
A_AAA_LORA = 64
A_GATE_LORA = 128
A_GN_EPS = 64e-5
A_DECAY_SCALE = 0.6065306597126334
A_SHIFT_COLS = 3 * A_WIDTH + A_DECAY_LORA + A_AAA_LORA + A_GATE_LORA

B_HEADS = 8
B_HEAD_DIM = 64
B_WIDTH = B_HEADS * B_HEAD_DIM

C_HEADS = 4
C_QK_DIM = 64
C_V_DIM = 2 * C_QK_DIM
C_QK_COLS = C_HEADS * 2 * C_QK_DIM
C_WIDTH = C_HEADS * C_V_DIM
ROPE_THETA = 500000.0
C_ROT_DIMS = C_QK_DIM // 4

D_FF = 2816
CONV_WIDTH = 3

LANES = 128
SUBLANES = 8

PA_OFF = 0
PA_PAD = 2048
PB_OFF = PA_PAD
PC_OFF = PB_OFF + 3 * B_WIDTH
PG_OFF = PC_OFF + 2 * C_QK_COLS + C_WIDTH
PROJ_COLS = PG_OFF + 3 * D_MODEL

VMEM_LIMIT = 56 * 1024 * 1024
ROW_TILE = 1024

EXP_ZERO_BELOW = -104.0


def _cparams(sem):
    return pltpu.CompilerParams(dimension_semantics=sem, vmem_limit_bytes=VMEM_LIMIT)


def _dot(a, b):
    return jnp.dot(a, b, preferred_element_type=F32)


def _dot_nt(a, b):
    return lax.dot_general(a, b, (((1,), (1,)), ((), ())), preferred_element_type=F32)


def _inproj_kernel(x_ref, g_ref, w_ref, o_ref, h_ref):
    @pl.when(pl.program_id(1) == 0)
    def _():
        x = x_ref[...]
        ms = jnp.mean(x * x, axis=-1, keepdims=True)
        h_ref[...] = (x * lax.rsqrt(ms + EPS) * g_ref[...]).astype(BF16)

    o_ref[...] = _dot(h_ref[...], w_ref[...]).astype(o_ref.dtype)


def in_projection(x2, g, w_pad, tm=1024, tn=1024):
    n = x2.shape[0]
    return pl.pallas_call(
        _inproj_kernel,
        grid=(n // tm, PROJ_COLS // tn),
        in_specs=[
            pl.BlockSpec((tm, D_MODEL), lambda i, j: (i, 0)),
            pl.BlockSpec((1, D_MODEL), lambda i, j: (0, 0)),
            pl.BlockSpec((D_MODEL, tn), lambda i, j: (0, j)),
        ],
        out_specs=pl.BlockSpec((tm, tn), lambda i, j: (i, j)),
        out_shape=jax.ShapeDtypeStruct((n, PROJ_COLS), BF16),
        scratch_shapes=[pltpu.VMEM((tm, D_MODEL), BF16)],
        compiler_params=_cparams(("parallel", "arbitrary")),
        name="in_projection",
    )(x2, g, w_pad)


def _rwkv_prep_kernel(pa_ref, prev_ref, mu_ref, w0_ref, wup_ref, a0_ref, aup_ref, gup_ref,
                      r_ref, k_ref, v_ref, lw_ref, cs_ref, a_ref, g_ref, *, chunk):
    tm = pa_ref.shape[0]
    cur = pa_ref[:, :A_SHIFT_COLS].astype(F32)
    prev_rows = prev_ref.shape[0]
    prev_last = prev_ref[...].astype(F32)[prev_rows - 1:prev_rows, :A_SHIFT_COLS]
    prev_last = jnp.where(pl.program_id(1) == 0, 0.0, prev_last)
    shifted = pltpu.roll(cur, 1, axis=0)
    row = lax.broadcasted_iota(jnp.int32, (tm, 1), 0)
    shifted = jnp.where(row == 0, prev_last, shifted)
    xs = cur + (shifted - cur) * mu_ref[...]

    w = A_WIDTH
    r = xs[:, 0:w]
    k = xs[:, w:2 * w]
    v = xs[:, 2 * w:3 * w]
    dw = xs[:, 3 * w:3 * w + A_DECAY_LORA]
    da = xs[:, 3 * w + A_DECAY_LORA:3 * w + A_DECAY_LORA + A_AAA_LORA]
    dg = xs[:, 3 * w + A_DECAY_LORA + A_AAA_LORA:A_SHIFT_COLS]

    lw = -A_DECAY_SCALE * jax.nn.sigmoid(w0_ref[...] + _dot(jnp.tanh(dw).astype(BF16), wup_ref[...]))
    a = jax.nn.sigmoid(a0_ref[...] + _dot(da.astype(BF16), aup_ref[...]))
    g = _dot(jax.nn.sigmoid(dg).astype(BF16), gup_ref[...])

    ti = lax.broadcasted_iota(jnp.int32, (tm, tm), 0)
    si = lax.broadcasted_iota(jnp.int32, (tm, tm), 1)
    tri = jnp.where((ti // chunk == si // chunk) & (si <= ti), 1.0, 0.0).astype(BF16)
    lw_hi = lw.astype(BF16)
    lw_lo = (lw - lw_hi.astype(F32)).astype(BF16)
    cs = _dot(tri, lw_hi) + _dot(tri, lw_lo)

    for h in range(A_HEADS):
        sl = slice(h * A_HEAD_DIM, (h + 1) * A_HEAD_DIM)
        r_ref[0, h] = r[:, sl]
        k_ref[0, h] = k[:, sl]
        v_ref[0, h] = v[:, sl]
        lw_ref[0, h] = lw[:, sl]
        cs_ref[0, h] = cs[:, sl]
        a_ref[0, h] = a[:, sl]
        g_ref[0, h] = g[:, sl]


def rwkv_prep(proj, bsz, t_len, mu, w0, w_up, a0, a_up, g_up, chunk, tm=256):
    tpb = t_len // tm
    prev_rows = 2 * SUBLANES
    rb = tm // prev_rows

    def prev_map(b, i):
        return (jnp.maximum((b * tpb + i) * rb - 1, 0), 0)

    head_shape = jax.ShapeDtypeStruct((bsz, A_HEADS, t_len, A_HEAD_DIM), F32)
    head_spec = pl.BlockSpec((1, A_HEADS, tm, A_HEAD_DIM), lambda b, i: (b, 0, i, 0))
    full = lambda shape: pl.BlockSpec(shape, lambda b, i: (0,) * len(shape))
    return pl.pallas_call(
        functools.partial(_rwkv_prep_kernel, chunk=chunk),
        grid=(bsz, tpb),
        in_specs=[
            pl.BlockSpec((tm, PA_PAD), lambda b, i: (b * tpb + i, 0)),
            pl.BlockSpec((prev_rows, PA_PAD), prev_map),
            full((1, A_SHIFT_COLS)),
            full((1, A_WIDTH)),
            full((A_DECAY_LORA, A_WIDTH)),
            full((1, A_WIDTH)),
            full((A_AAA_LORA, A_WIDTH)),
            full((A_GATE_LORA, A_WIDTH)),
        ],
        out_specs=[head_spec] * 7,
        out_shape=[head_shape] * 7,
        compiler_params=_cparams(("parallel", "arbitrary")),
        name="rwkv_prep",
    )(proj, proj, mu, w0, w_up, a0, a_up, g_up)


def _rwkv_scan_kernel(r_ref, k_ref, v_ref, lw_ref, cs_ref, a_ref, g_ref, kk_ref, ka_ref, rk_ref,
                      lnw_ref, lnb_ref, o_ref, s_ref, *, chunk):
    c = chunk
    d = A_HEAD_DIM
    heads = range(A_HEADS)
    bf = lambda z: z.astype(BF16)

    @pl.when(pl.program_id(1) == 0)
    def _():
        s_ref[...] = jnp.zeros_like(s_ref)

    ti = lax.broadcasted_iota(jnp.int32, (c, c), 0)
    si = lax.broadcasted_iota(jnp.int32, (c, c), 1)
    strict = si < ti
    incl = si <= ti
    n_double = int(np.ceil(np.log2(c)))

    r = [r_ref[0, h] for h in heads]
    v = [v_ref[0, h] for h in heads]
    vb = [bf(z) for z in v]
    kmod, at, bt, kt, rt, gend = [], [], [], [], [], []
    for h in heads:
        k = k_ref[0, h]
        a = a_ref[0, h]
        cs = cs_ref[0, h]
        kk = k * kk_ref[h]
        kk = kk * lax.rsqrt(jnp.maximum(jnp.sum(kk * kk, axis=-1, keepdims=True), 1e-24))
        kmod.append(k * (1.0 + (a - 1.0) * ka_ref[h]))
        gam = jnp.exp(cs)
        ginv = jnp.exp(-cs)
        at.append(-kk * jnp.exp(cs - lw_ref[0, h]))
        bt.append(kk * a * ginv)
        kt.append(kmod[h] * ginv)
        rt.append(r[h] * gam)
        gend.append(gam[c - 1:c, :])

    m_ab, p_rb, m_ak, p_rk = [], [], [], []
    for h in heads:
        lhs = bf(jnp.concatenate([at[h], rt[h]], axis=0))
        fb = _dot_nt(lhs, bf(bt[h]))
        fk = _dot_nt(lhs, bf(kt[h]))
        m_ab.append(jnp.where(strict, fb[:c], 0.0))
        p_rb.append(bf(jnp.where(incl, fb[c:], 0.0)))
        m_ak.append(bf(jnp.where(strict, fk[:c], 0.0)))
        p_rk.append(bf(jnp.where(incl, fk[c:], 0.0)))

    x = [jnp.concatenate([at[h], _dot(m_ak[h], vb[h])], axis=1) for h in heads]
    p = m_ab
    for j in range(n_double):
        pb = [bf(z) for z in p]
        x = [x[h] + _dot(pb[h], bf(x[h])) for h in heads]
        if j + 1 < n_double:
            p = [_dot(pb[h], pb[h]) for h in heads]

    outs = []
    for h in heads:
        s0 = s_ref[h]
        s0b = bf(s0)
        xh = bf(x[h])
        y = _dot(p_rb[h], xh)
        o = _dot_nt(bf(y[:, :d] + rt[h]), s0b) + y[:, d:] + _dot(p_rk[h], vb[h])

        xb = _dot(bf(x[h].T), bf(bt[h] * gend[h]))
        vk = _dot(bf(v[h].T), bf(kt[h] * gend[h]))
        s_ref[h] = s0 * gend[h] + _dot(s0b, bf(xb[:d])) + xb[d:] + vk

        mean = jnp.mean(o, axis=-1, keepdims=True)
        var = jnp.mean(jnp.square(o - mean), axis=-1, keepdims=True)
        o = (o - mean) * lax.rsqrt(var + A_GN_EPS)
        o = o * lnw_ref[h] + lnb_ref[h]
        bonus = jnp.sum(r[h] * kmod[h] * rk_ref[h], axis=-1, keepdims=True) * v[h]
        outs.append((o + bonus) * g_ref[0, h])

    o_ref[0] = jnp.concatenate(outs, axis=1)


def rwkv_scan(r, k, v, lw, cs, a, g, k_k, k_a, r_k, ln_w, ln_b, chunk):
    bsz, _, t_len, _ = r.shape
    head_spec = pl.BlockSpec((1, A_HEADS, chunk, A_HEAD_DIM), lambda b, i: (b, 0, i, 0))
    par_spec = pl.BlockSpec((A_HEADS, 1, A_HEAD_DIM), lambda b, i: (0, 0, 0))
    per_head = lambda z: z.reshape(A_HEADS, 1, A_HEAD_DIM)
    return pl.pallas_call(
        functools.partial(_rwkv_scan_kernel, chunk=chunk),
        grid=(bsz, t_len // chunk),
        in_specs=[head_spec] * 7 + [par_spec] * 5,
        out_specs=pl.BlockSpec((1, chunk, A_WIDTH), lambda b, i: (b, i, 0)),
        out_shape=jax.ShapeDtypeStruct((bsz, t_len, A_WIDTH), F32),
        scratch_shapes=[pltpu.VMEM((A_HEADS, A_HEAD_DIM, A_HEAD_DIM), F32)],
        compiler_params=_cparams(("parallel", "arbitrary")),
        name="rwkv_scan",
    )(r, k, v, lw, cs, a, g, per_head(k_k), per_head(k_a), per_head(r_k), per_head(ln_w),
      per_head(ln_b))


def _sb_attn_kernel(q_ref, k_ref, v_ref, o_ref, *, blk):
    i = pl.program_id(1)
    scale = B_HEAD_DIM ** -0.5
    lane = lax.broadcasted_iota(jnp.int32, (1, LANES), 1)
    head0 = lane < B_HEAD_DIM
    pairs = B_WIDTH // LANES
    zero_q = jnp.zeros((blk, LANES), BF16)
    qh = []
    for p in range(pairs):
        qp = q_ref[:, p * LANES:(p + 1) * LANES] * scale
        qh += [jnp.where(head0, qp, zero_q), jnp.where(head0, zero_q, qp)]

    ju = lax.broadcasted_iota(jnp.int32, (blk, blk + LANES), 0)
    su = lax.broadcasted_iota(jnp.int32, (blk, blk + LANES), 1)
    u = jnp.where((ju > su) | (su >= blk), 1.0, 0.0).astype(BF16)
    u2 = jnp.concatenate([u, u], axis=0)

    def block(kb, cs, accs, diagonal):
        start = pl.multiple_of(kb * blk, blk)
        if diagonal:
            mask = (lax.broadcasted_iota(jnp.int32, (blk, blk), 1)
                    < lax.broadcasted_iota(jnp.int32, (blk, blk), 0))
        heads = range(B_HEADS)
        cols = [pl.ds((h // 2) * LANES, LANES) for h in heads]
        z = [_dot_nt(qh[h], k_ref[pl.ds(start, blk), cols[h]]) for h in heads]
        lbeta, lu = [], []
        for h in heads:
            sp = jnp.maximum(z[h], 0.0) + jnp.log(1.0 + jnp.exp(-jnp.abs(z[h])))
            lbeta.append(z[h] - sp)
            skip = jnp.where(mask, sp, 0.0) if diagonal else sp
            s_hi = skip.astype(BF16)
            s_lo = (skip - s_hi.astype(F32)).astype(BF16)
            lu.append(_dot(jnp.concatenate([s_hi, s_lo], axis=1), u2))
        new_cs, new_accs = [], []
        for h in heads:
            amat = jnp.exp(lbeta[h] - lu[h][:, :blk] + cs[h])
            if diagonal:
                amat = jnp.where(mask, amat, 0.0)
            new_accs.append(accs[h] + _dot(amat.astype(BF16), v_ref[pl.ds(start, blk), cols[h]]))
            new_cs.append(cs[h] - lu[h][:, blk:])
        return new_cs, new_accs

    def cmax_of(cs):
        m = cs[0]
        for z in cs[1:]:
            m = jnp.maximum(m, z)
        return jnp.max(m)

    def cond(carry):
        return jnp.logical_and(carry[0] >= 0, carry[1] > EXP_ZERO_BELOW)

    def body(carry):
        kb = carry[0]
        cs, accs = block(kb, list(carry[2:2 + B_HEADS]), list(carry[2 + B_HEADS:]), diagonal=False)
        return (kb - 1, cmax_of(cs), *cs, *accs)

    zeros = [jnp.zeros((blk, LANES), F32)] * B_HEADS
    cs, accs = block(i, zeros, zeros, diagonal=True)
    out = lax.while_loop(cond, body, (i - 1, cmax_of(cs), *cs, *accs))
    accs = out[2 + B_HEADS:]
    for p in range(pairs):
        o_ref[:, p * LANES:(p + 1) * LANES] = jnp.where(head0, accs[2 * p], accs[2 * p + 1])


def stick_breaking_attention(proj, bsz, t_len, blk=128):
    assert blk == LANES
    nq = t_len // blk
    qc, kc, vc = (PB_OFF // B_WIDTH, PB_OFF // B_WIDTH + 1, PB_OFF // B_WIDTH + 2)
    return pl.pallas_call(
        functools.partial(_sb_attn_kernel, blk=blk),
        grid=(bsz, nq),
        in_specs=[
            pl.BlockSpec((blk, B_WIDTH), lambda b, i: (b * nq + i, qc)),
            pl.BlockSpec((t_len, B_WIDTH), lambda b, i: (b, kc)),
            pl.BlockSpec((t_len, B_WIDTH), lambda b, i: (b, vc)),
        ],
        out_specs=pl.BlockSpec((blk, B_WIDTH), lambda b, i: (b * nq + i, 0)),
        out_shape=jax.ShapeDtypeStruct((bsz * t_len, B_WIDTH), F32),
        compiler_params=_cparams(("parallel", "arbitrary")),
        name="stick_breaking_attention",
    )(proj, proj, proj)


def _rope_kernel(q_ref, k_ref, cos_ref, sinm_ref, sinp_ref, qo_ref, ko_ref):
    cos, sinm, sinp = cos_ref[...], sinm_ref[...], sinp_ref[...]
    half = C_ROT_DIMS // 2
    scale = C_QK_DIM ** -0.5

    def rot(x):
        return x * cos + pltpu.roll(x, LANES - half, axis=1) * sinm + pltpu.roll(x, half, axis=1) * sinp

    for h in range(C_HEADS):
        sl = slice(h * LANES, (h + 1) * LANES)
        qo_ref[:, sl] = (rot(q_ref[:, sl].astype(F32)) * scale).astype(BF16)
        ko_ref[:, sl] = rot(k_ref[:, sl].astype(F32)).astype(BF16)


def rope_qk(proj, bsz, t_len, cos_t, sinm_t, sinp_t, tm=512):
    n = bsz * t_len
    tpb = t_len // tm
    qc, kc = PC_OFF // C_QK_COLS, (PC_OFF + C_QK_COLS) // C_QK_COLS
    tab = pl.BlockSpec((tm, LANES), lambda i: (i % tpb, 0))
    out = jax.ShapeDtypeStruct((n, C_QK_COLS), BF16)
    return pl.pallas_call(
        _rope_kernel,
        grid=(n // tm,),
        in_specs=[
            pl.BlockSpec((tm, C_QK_COLS), lambda i: (i, qc)),
            pl.BlockSpec((tm, C_QK_COLS), lambda i: (i, kc)),
            tab, tab, tab,
        ],
        out_specs=[pl.BlockSpec((tm, C_QK_COLS), lambda i: (i, 0))] * 2,
        out_shape=[out, out],
        compiler_params=_cparams(("parallel",)),
        name="rope_qk",
    )(proj, proj, cos_t, sinm_t, sinp_t)


def _diff_attn_kernel(q_ref, k_ref, v_ref, lam_ref, g_ref, o_ref, *, blk, lambda_init):
    i = pl.program_id(1)
    lane = lax.broadcasted_iota(jnp.int32, (1, LANES), 1)
    half0 = lane < C_QK_DIM
    zero_q = jnp.zeros((blk, LANES), BF16)
    chains = [(h, s) for h in range(C_HEADS) for s in range(2)]
    qc = []
    for h in range(C_HEADS):
        q = q_ref[:, h * LANES:(h + 1) * LANES]
        qc += [jnp.where(half0, q, zero_q), jnp.where(half0, zero_q, q)]
    ones = jnp.ones((blk, LANES), BF16)
    group = 4

    def step(kb, state, diagonal):
        ms, accs = state
        start = pl.multiple_of(kb * blk, blk)
        rows = pl.ds(start, blk)
        hcols = [pl.ds(h * LANES, LANES) for h in range(C_HEADS)]
        if diagonal:
            mask = (lax.broadcasted_iota(jnp.int32, (blk, blk), 1) // CHUNK
                    <= lax.broadcasted_iota(jnp.int32, (blk, blk), 0) // CHUNK)
        new_ms, new_accs = [None] * len(chains), [None] * len(chains)
        for g0 in range(0, len(chains), group):
            grp = range(g0, g0 + group)
            z = {c: _dot_nt(qc[c], k_ref[rows, hcols[chains[c][0]]]) for c in grp}
            if diagonal:
                z = {c: jnp.where(mask, z[c], -jnp.inf) for c in grp}
            alphas, ps = {}, {}
            for c in grp:
                m_new = jnp.maximum(ms[c], jnp.max(z[c], axis=-1, keepdims=True))
                alphas[c] = jnp.exp(ms[c] - m_new)
                ps[c] = jnp.exp(z[c] - m_new).astype(BF16)
                new_ms[c] = m_new
            for c in grp:
                vext = jnp.concatenate([v_ref[rows, hcols[chains[c][0]]], ones], axis=1)
                new_accs[c] = alphas[c] * accs[c] + _dot(ps[c], vext)
        return new_ms, new_accs

    ms0 = [jnp.full((blk, 1), -1e30, F32)] * len(chains)
    accs0 = [jnp.zeros((blk, 2 * LANES), F32)] * len(chains)
    nc = len(chains)

    def body(kb, carry):
        ms, accs = step(kb, (list(carry[:nc]), list(carry[nc:])), diagonal=False)
        return (*ms, *accs)

    carry = lax.fori_loop(0, i, body, (*ms0, *accs0))
    _, accs = step(i, (list(carry[:nc]), list(carry[nc:])), diagonal=True)

    lv = lam_ref[...]
    lam = (jnp.exp(jnp.sum(lv[0:1] * lv[1:2], axis=-1, keepdims=True))
           - jnp.exp(jnp.sum(lv[2:3] * lv[3:4], axis=-1, keepdims=True)) + lambda_init)
    for h in range(C_HEADS):
        a1, a2 = accs[2 * h], accs[2 * h + 1]
        o = a1[:, :LANES] / a1[:, LANES:] - lam * (a2[:, :LANES] / a2[:, LANES:])
        o = o * lax.rsqrt(jnp.mean(o * o, axis=-1, keepdims=True) + EPS) * g_ref[...]
        o_ref[:, h * LANES:(h + 1) * LANES] = o * (1.0 - lambda_init)


def differential_attention(q_rot, k_rot, proj, bsz, t_len, lam_vecs, subln_g, lambda_init, blk=256):
    nq = t_len // blk
    vc = (PC_OFF + 2 * C_QK_COLS) // C_WIDTH
    return pl.pallas_call(
        functools.partial(_diff_attn_kernel, blk=blk, lambda_init=lambda_init),
        grid=(bsz, nq),
        in_specs=[
            pl.BlockSpec((blk, C_QK_COLS), lambda b, i: (b * nq + i, 0)),
            pl.BlockSpec((t_len, C_QK_COLS), lambda b, i: (b, 0)),
            pl.BlockSpec((t_len, C_WIDTH), lambda b, i: (b, vc)),
            pl.BlockSpec((4, C_QK_DIM), lambda b, i: (0, 0)),
            pl.BlockSpec((1, C_V_DIM), lambda b, i: (0, 0)),
        ],
        out_specs=pl.BlockSpec((blk, C_WIDTH), lambda b, i: (b * nq + i, 0)),
        out_shape=jax.ShapeDtypeStruct((bsz * t_len, C_WIDTH), F32),
        compiler_params=_cparams(("parallel", "arbitrary")),
        name="differential_attention",
    )(q_rot, k_rot, proj, lam_vecs, subln_g)


def _merge_kernel(x_ref, oa_ref, ob_ref, oc_ref, ga_ref, gb_ref, gc_ref, wa_ref, wb_ref, wc_ref,
                  wo_ref, o_ref):
    def branch(o_r, g_r, w_r):
        return jax.nn.sigmoid(g_r[...].astype(F32)) * _dot(o_r[...].astype(BF16), w_r[...])

    merged = (branch(oa_ref, ga_ref, wa_ref) + branch(ob_ref, gb_ref, wb_ref)
              + branch(oc_ref, gc_ref, wc_ref))
    o_ref[...] = x_ref[...] + _dot(merged.astype(BF16), wo_ref[...])


def merge_branches(x2, oa, ob, oc, proj, wa, wb, wc, wo, tm=512):
    n = x2.shape[0]
    g0 = PG_OFF // D_MODEL
    row = lambda width: pl.BlockSpec((tm, width), lambda i: (i, 0))
    gate = lambda j: pl.BlockSpec((tm, D_MODEL), lambda i: (i, g0 + j))
    wspec = lambda rows: pl.BlockSpec((rows, D_MODEL), lambda i: (0, 0))
    return pl.pallas_call(
        _merge_kernel,
        grid=(n // tm,),
        in_specs=[row(D_MODEL), row(A_WIDTH), row(B_WIDTH), row(C_WIDTH), gate(0), gate(1), gate(2),
                  wspec(A_WIDTH), wspec(B_WIDTH), wspec(C_WIDTH), wspec(D_MODEL)],
        out_specs=row(D_MODEL),
        out_shape=jax.ShapeDtypeStruct((n, D_MODEL), F32),
        compiler_params=_cparams(("parallel",)),
        name="merge_branches",
    )(x2, oa, ob, oc, proj, proj, proj, wa, wb, wc, wo)


def _ffn_kernel(x_ref, xp_ref, g_ref, wg_ref, wv_ref, cwg_ref, cwv_ref, cbg_ref, cbv_ref, wd_ref,
                gf_ref, o_ref, h_ref, u_ref, acc_ref, *, final_norm):
    tm = x_ref.shape[0]
    tf = wg_ref.shape[1]
    halo = SUBLANES
    j = pl.program_id(2)

    def norm(x):
        return x * lax.rsqrt(jnp.mean(x * x, axis=-1, keepdims=True) + EPS)

    @pl.when(j == 0)
    def _():
        g = g_ref[...]
        h_ref[halo:, :] = (norm(x_ref[...]) * g).astype(BF16)
        hp = norm(xp_ref[...]) * g
        h_ref[:halo, :] = jnp.where(pl.program_id(1) == 0, 0.0, hp).astype(BF16)
        acc_ref[...] = jnp.zeros_like(acc_ref)

    h = h_ref[...]
    u_ref[:, :tf] = _dot(h, wg_ref[...])
    u_ref[:, tf:] = _dot(h, wv_ref[...])

    def conv(lo, cw_ref, cb_ref):
        out = cb_ref[...]
        for tap in range(CONV_WIDTH):
            rows = pl.ds(halo - (CONV_WIDTH - 1 - tap), tm)
            out = out + cw_ref[tap:tap + 1, :] * u_ref[rows, lo:lo + tf]
        return out

    gate = conv(0, cwg_ref, cbg_ref)
    val = conv(tf, cwv_ref, cbv_ref)
    act = (gate * jax.nn.sigmoid(gate) * val).astype(BF16)
    acc_ref[...] += _dot(act, wd_ref[...])

    @pl.when(j == pl.num_programs(2) - 1)
    def _():
        y = x_ref[...] + acc_ref[...]
        if final_norm:
            y = norm(y) * gf_ref[...]
        o_ref[...] = y


def conv_ffn(x2, bsz, t_len, g, w_up, conv_w, conv_b, w_down, g_final, final_norm, tm=1024, tf=256):
    n = x2.shape[0]
    tpb = t_len // tm
    nf = D_FF // tf
    rb = tm // SUBLANES

    def prev_map(b, i, j):
        return (jnp.maximum((b * tpb + i) * rb - 1, 0), 0)

    return pl.pallas_call(
        functools.partial(_ffn_kernel, final_norm=final_norm),
        grid=(bsz, tpb, nf),
        in_specs=[
            pl.BlockSpec((tm, D_MODEL), lambda b, i, j: (b * tpb + i, 0)),
            pl.BlockSpec((SUBLANES, D_MODEL), prev_map),
            pl.BlockSpec((1, D_MODEL), lambda b, i, j: (0, 0)),
            pl.BlockSpec((D_MODEL, tf), lambda b, i, j: (0, j)),
            pl.BlockSpec((D_MODEL, tf), lambda b, i, j: (0, nf + j)),
            pl.BlockSpec((CONV_WIDTH, tf), lambda b, i, j: (0, j)),
            pl.BlockSpec((CONV_WIDTH, tf), lambda b, i, j: (0, nf + j)),
            pl.BlockSpec((1, tf), lambda b, i, j: (0, j)),
            pl.BlockSpec((1, tf), lambda b, i, j: (0, nf + j)),
            pl.BlockSpec((tf, D_MODEL), lambda b, i, j: (j, 0)),
            pl.BlockSpec((1, D_MODEL), lambda b, i, j: (0, 0)),
        ],
        out_specs=pl.BlockSpec((tm, D_MODEL), lambda b, i, j: (b * tpb + i, 0)),
        out_shape=jax.ShapeDtypeStruct((n, D_MODEL), F32),
        scratch_shapes=[
            pltpu.VMEM((tm + SUBLANES, D_MODEL), BF16),
            pltpu.VMEM((tm + SUBLANES, 2 * tf), F32),
            pltpu.VMEM((tm, D_MODEL), F32),
        ],
        compiler_params=_cparams(("parallel", "parallel", "arbitrary")),
        name="conv_ffn",
    )(x2, x2, g, w_up, w_up, conv_w, conv_w, conv_b, conv_b, w_down, g_final)


def _rope_tables(t_len):
    half = C_ROT_DIMS // 2
    inv_freq = ROPE_THETA ** (-jnp.arange(0, C_ROT_DIMS, 2, dtype=F32) / C_ROT_DIMS)
    ang = jnp.arange(t_len, dtype=F32)[:, None] * inv_freq[None, :]
    cos, sin = jnp.cos(ang), jnp.sin(ang)
    rest = C_QK_DIM - C_ROT_DIMS
    one = jnp.ones((t_len, rest), F32)
    zero = jnp.zeros((t_len, rest), F32)
    zh = jnp.zeros((t_len, half), F32)
    cos_t = jnp.concatenate([cos, cos, one], axis=1)
    sinm_t = jnp.concatenate([-sin, zh, zero], axis=1)
    sinp_t = jnp.concatenate([zh, sin, zero], axis=1)
    dup = lambda z: jnp.concatenate([z, z], axis=1)
    return dup(cos_t), dup(sinm_t), dup(sinp_t)


def _pad_w_in(w):
    pad = jnp.zeros((D_MODEL, PA_PAD - A_SHIFT_COLS), w.dtype)
    return jnp.concatenate([w[:, :A_SHIFT_COLS], pad, w[:, A_SHIFT_COLS:]], axis=1).astype(BF16)


def kernel(x, norm_mix_g, norm_ffn_g, w_in, rwkv_mu, rwkv_w0, rwkv_w_up, rwkv_a0, rwkv_a_up,
           rwkv_g_up, rwkv_k_k, rwkv_k_a, rwkv_r_k, rwkv_ln_w, rwkv_ln_b, diff_lambda, diff_subln_g,
           w_branch_a, w_branch_b, w_branch_c, w_out, ffn_w_up, ffn_conv_w, ffn_conv_b, ffn_w_down,
           norm_final_g):
    bsz, t_len, d_model = x.shape
    assert d_model == D_MODEL and t_len % ROW_TILE == 0, (x.shape, ROW_TILE)
    n = bsz * t_len
    cos_t, sinm_t, sinp_t = _rope_tables(t_len)
    row = lambda z: z.reshape(1, -1)
    x2 = x.reshape(n, D_MODEL)

    for l in range(DEPTH):
        proj = in_projection(x2, row(norm_mix_g[l]), _pad_w_in(w_in[l]))

        r, k, v, lw, cs, a, g = rwkv_prep(
            proj, bsz, t_len, row(rwkv_mu[l]), row(rwkv_w0[l]), rwkv_w_up[l].astype(BF16),
            row(rwkv_a0[l]), rwkv_a_up[l].astype(BF16), rwkv_g_up[l].astype(BF16), RWKV_CHUNK)
        oa = rwkv_scan(r, k, v, lw, cs, a, g, rwkv_k_k[l], rwkv_k_a[l], rwkv_r_k[l].reshape(-1),
                       rwkv_ln_w[l], rwkv_ln_b[l], RWKV_CHUNK).reshape(n, A_WIDTH)

        ob = stick_breaking_attention(proj, bsz, t_len)

        lambda_init = 0.8 - 0.6 * float(np.exp(-0.3 * l))
        q_rot, k_rot = rope_qk(proj, bsz, t_len, cos_t, sinm_t, sinp_t)
        oc = differential_attention(q_rot, k_rot, proj, bsz, t_len, diff_lambda[l],
                                    row(diff_subln_g[l]), lambda_init)

        x2 = merge_branches(x2, oa, ob, oc, proj, w_branch_a[l].astype(BF16),
                            w_branch_b[l].astype(BF16), w_branch_c[l].astype(BF16),
                            w_out[l].astype(BF16))

        x2 = conv_ffn(x2, bsz, t_len, row(norm_ffn_g[l]), ffn_w_up[l].astype(BF16), ffn_conv_w[l],
                      row(ffn_conv_b[l]), ffn_w_down[l].astype(BF16), row(norm_final_g),
                      final_norm=(l == DEPTH - 1))

    return x2.reshape(bsz, t_len, D_MODEL)
```

```python
import functools

import numpy as np
import jax
import jax.numpy as jnp
from jax import lax
from jax.experimental import pallas as pl
from jax.experimental.pallas import tpu as pltpu

F32 = jnp.float32
BF16 = jnp.bfloat16

D_MODEL = 1024
DEPTH = 4
CHUNK = 64
RWKV_CHUNK = 64
EPS = 1e-6

A_HEADS = 8
A_HEAD_DIM = 64
A_WIDTH = A_HEADS * A_HEAD_DIM
A_DECAY_LORA = 64
A_AAA_LORA = 64
A_GATE_LORA = 128
A_GN_EPS = 64e-5
A_DECAY_SCALE = 0.6065306597126334
A_SHIFT_COLS = 3 * A_WIDTH + A_DECAY_LORA + A_AAA_LORA + A_GATE_LORA

B_HEADS = 8
B_HEAD_DIM = 64
B_WIDTH = B_HEADS * B_HEAD_DIM

C_HEADS = 4
C_QK_DIM = 64
C_V_DIM = 2 * C_QK_DIM
C_QK_COLS = C_HEADS * 2 * C_QK_DIM
C_WIDTH = C_HEADS * C_V_DIM
ROPE_THETA = 500000.0
C_ROT_DIMS = C_QK_DIM // 4

D_FF = 2816
CONV_WIDTH = 3

LANES = 128
SUBLANES = 8

PA_OFF = 0
PA_PAD = 2048
PB_OFF = PA_PAD
PC_OFF = PB_OFF + 3 * B_WIDTH
PG_OFF = PC_OFF + 2 * C_QK_COLS + C_WIDTH
PROJ_COLS = PG_OFF + 3 * D_MODEL

VMEM_LIMIT = 56 * 1024 * 1024
ROW_TILE = 1024
FFN_HALO = 2 * SUBLANES

EXP_ZERO_BELOW = -104.0


def _cparams(sem):
    return pltpu.CompilerParams(dimension_semantics=sem, vmem_limit_bytes=VMEM_LIMIT)


def _dot(a, b):
    return jnp.dot(a, b, preferred_element_type=F32)


def _dot_nt(a, b):
    return lax.dot_general(a, b, (((1,), (1,)), ((), ())), preferred_element_type=F32)


def _inproj_kernel(x_ref, g_ref, w_ref, o_ref, h_ref):
    @pl.when(pl.program_id(1) == 0)
    def _():
        x = x_ref[...]
        ms = jnp.mean(x * x, axis=-1, keepdims=True)
        h_ref[...] = (x * lax.rsqrt(ms + EPS) * g_ref[...]).astype(BF16)

    o_ref[...] = _dot(h_ref[...], w_ref[...]).astype(o_ref.dtype)


def in_projection(x2, g, w_pad, tm=1024, tn=1024):
    n = x2.shape[0]
    return pl.pallas_call(
        _inproj_kernel,
        grid=(n // tm, PROJ_COLS // tn),
        in_specs=[
            pl.BlockSpec((tm, D_MODEL), lambda i, j: (i, 0)),
            pl.BlockSpec((1, D_MODEL), lambda i, j: (0, 0)),
            pl.BlockSpec((D_MODEL, tn), lambda i, j: (0, j)),
        ],
        out_specs=pl.BlockSpec((tm, tn), lambda i, j: (i, j)),
        out_shape=jax.ShapeDtypeStruct((n, PROJ_COLS), BF16),
        scratch_shapes=[pltpu.VMEM((tm, D_MODEL), BF16)],
        compiler_params=_cparams(("parallel", "arbitrary")),
        name="in_projection",
    )(x2, g, w_pad)


def _rwkv_prep_kernel(pa_ref, prev_ref, mu_ref, w0_ref, wup_ref, a0_ref, aup_ref, gup_ref,
                      r_ref, k_ref, v_ref, lw_ref, cs_ref, a_ref, g_ref, *, chunk):
    tm = pa_ref.shape[0]
    cur = pa_ref[:, :A_SHIFT_COLS].astype(F32)
    prev_rows = prev_ref.shape[0]
    prev_last = prev_ref[...].astype(F32)[prev_rows - 1:prev_rows, :A_SHIFT_COLS]
    prev_last = jnp.where(pl.program_id(1) == 0, 0.0, prev_last)
    shifted = pltpu.roll(cur, 1, axis=0)
    row = lax.broadcasted_iota(jnp.int32, (tm, 1), 0)
    shifted = jnp.where(row == 0, prev_last, shifted)
    xs = cur + (shifted - cur) * mu_ref[...]

    w = A_WIDTH
    r = xs[:, 0:w]
    k = xs[:, w:2 * w]
    v = xs[:, 2 * w:3 * w]
    dw = xs[:, 3 * w:3 * w + A_DECAY_LORA]
    da = xs[:, 3 * w + A_DECAY_LORA:3 * w + A_DECAY_LORA + A_AAA_LORA]
    dg = xs[:, 3 * w + A_DECAY_LORA + A_AAA_LORA:A_SHIFT_COLS]

    lw = -A_DECAY_SCALE * jax.nn.sigmoid(w0_ref[...] + _dot(jnp.tanh(dw).astype(BF16), wup_ref[...]))
    a = jax.nn.sigmoid(a0_ref[...] + _dot(da.astype(BF16), aup_ref[...]))
    g = _dot(jax.nn.sigmoid(dg).astype(BF16), gup_ref[...])

    ti = lax.broadcasted_iota(jnp.int32, (tm, tm), 0)
    si = lax.broadcasted_iota(jnp.int32, (tm, tm), 1)
    tri = jnp.where((ti // chunk == si // chunk) & (si <= ti), 1.0, 0.0).astype(BF16)
    lw_hi = lw.astype(BF16)
    lw_lo = (lw - lw_hi.astype(F32)).astype(BF16)
    cs = _dot(tri, lw_hi) + _dot(tri, lw_lo)

    for h in range(A_HEADS):
        sl = slice(h * A_HEAD_DIM, (h + 1) * A_HEAD_DIM)
        r_ref[0, h] = r[:, sl]
        k_ref[0, h] = k[:, sl]
        v_ref[0, h] = v[:, sl]
        lw_ref[0, h] = lw[:, sl]
        cs_ref[0, h] = cs[:, sl]
        a_ref[0, h] = a[:, sl]
        g_ref[0, h] = g[:, sl]


def rwkv_prep(proj, bsz, t_len, mu, w0, w_up, a0, a_up, g_up, chunk, tm=256):
    tpb = t_len // tm
    prev_rows = 2 * SUBLANES
    rb = tm // prev_rows

    def prev_map(b, i):
        return (jnp.maximum((b * tpb + i) * rb - 1, 0), 0)

    head_shape = jax.ShapeDtypeStruct((bsz, A_HEADS, t_len, A_HEAD_DIM), F32)
    head_spec = pl.BlockSpec((1, A_HEADS, tm, A_HEAD_DIM), lambda b, i: (b, 0, i, 0))
    full = lambda shape: pl.BlockSpec(shape, lambda b, i: (0,) * len(shape))
    return pl.pallas_call(
        functools.partial(_rwkv_prep_kernel, chunk=chunk),
        grid=(bsz, tpb),
        in_specs=[
            pl.BlockSpec((tm, PA_PAD), lambda b, i: (b * tpb + i, 0)),
            pl.BlockSpec((prev_rows, PA_PAD), prev_map),
            full((1, A_SHIFT_COLS)),
            full((1, A_WIDTH)),
            full((A_DECAY_LORA, A_WIDTH)),
            full((1, A_WIDTH)),
            full((A_AAA_LORA, A_WIDTH)),
            full((A_GATE_LORA, A_WIDTH)),
        ],
        out_specs=[head_spec] * 7,
        out_shape=[head_shape] * 7,
        compiler_params=_cparams(("parallel", "arbitrary")),
        name="rwkv_prep",
    )(proj, proj, mu, w0, w_up, a0, a_up, g_up)


def _rwkv_scan_kernel(r_ref, k_ref, v_ref, lw_ref, cs_ref, a_ref, g_ref, kk_ref, ka_ref, rk_ref,
                      lnw_ref, lnb_ref, o_ref, s_ref, *, chunk):
    c = chunk
    d = A_HEAD_DIM
    nb = r_ref.shape[0]
    units = [(b, h) for b in range(nb) for h in range(A_HEADS)]
    idx = range(len(units))
    bf = lambda z: z.astype(BF16)

    @pl.when(pl.program_id(1) == 0)
    def _():
        s_ref[...] = jnp.zeros_like(s_ref)

    ti = lax.broadcasted_iota(jnp.int32, (c, c), 0)
    si = lax.broadcasted_iota(jnp.int32, (c, c), 1)
    strict = si < ti
    incl = si <= ti
    n_double = int(np.ceil(np.log2(c)))

    r = [r_ref[b, h] for b, h in units]
    v = [v_ref[b, h] for b, h in units]
    vb = [bf(z) for z in v]
    kmod, at, bt, kt, rt, gend = [], [], [], [], [], []
    for u, (b, h) in enumerate(units):
        k = k_ref[b, h]
        a = a_ref[b, h]
        cs = cs_ref[b, h]
        kk = k * kk_ref[h]
        kk = kk * lax.rsqrt(jnp.maximum(jnp.sum(kk * kk, axis=-1, keepdims=True), 1e-24))
        kmod.append(k * (1.0 + (a - 1.0) * ka_ref[h]))
        gam = jnp.exp(cs)
        ginv = jnp.exp(-cs)
        at.append(-kk * jnp.exp(cs - lw_ref[b, h]))
        bt.append(kk * a * ginv)
        kt.append(kmod[u] * ginv)
        rt.append(r[u] * gam)
        gend.append(gam[c - 1:c, :])

    m_ab, p_rb, m_ak, p_rk = [], [], [], []
    for u in idx:
        lhs = bf(jnp.concatenate([at[u], rt[u]], axis=0))
        fb = _dot_nt(lhs, bf(bt[u]))
        fk = _dot_nt(lhs, bf(kt[u]))
        m_ab.append(jnp.where(strict, fb[:c], 0.0))
        p_rb.append(bf(jnp.where(incl, fb[c:], 0.0)))
        m_ak.append(bf(jnp.where(strict, fk[:c], 0.0)))
        p_rk.append(bf(jnp.where(incl, fk[c:], 0.0)))

    x = [jnp.concatenate([at[u], _dot(m_ak[u], vb[u])], axis=1) for u in idx]
    p = m_ab
    for j in range(n_double):
        pb = [bf(z) for z in p]
        x = [x[u] + _dot(pb[u], bf(x[u])) for u in idx]
        if j + 1 < n_double:
            p = [_dot(pb[u], pb[u]) for u in idx]

    outs = []
    for u, (b, h) in enumerate(units):
        s0 = s_ref[u]
        s0b = bf(s0)
        xh = bf(x[u])
        y = _dot(p_rb[u], xh)
        o = _dot_nt(bf(y[:, :d] + rt[u]), s0b) + y[:, d:] + _dot(p_rk[u], vb[u])

        xb = _dot(bf(x[u].T), bf(bt[u] * gend[u]))
        vk = _dot(bf(v[u].T), bf(kt[u] * gend[u]))
        s_ref[u] = s0 * gend[u] + _dot(s0b, bf(xb[:d])) + xb[d:] + vk

        mean = jnp.mean(o, axis=-1, keepdims=True)
        var = jnp.mean(jnp.square(o - mean), axis=-1, keepdims=True)
        o = (o - mean) * lax.rsqrt(var + A_GN_EPS)
        o = o * lnw_ref[h] + lnb_ref[h]
        bonus = jnp.sum(r[u] * kmod[u] * rk_ref[h], axis=-1, keepdims=True) * v[u]
        outs.append((o + bonus) * g_ref[b, h])

    for b in range(nb):
        o_ref[b] = jnp.concatenate(outs[b * A_HEADS:(b + 1) * A_HEADS], axis=1)


def rwkv_scan(r, k, v, lw, cs, a, g, k_k, k_a, r_k, ln_w, ln_b, chunk, nb=4):
    bsz, _, t_len, _ = r.shape
    assert bsz % nb == 0 and t_len % chunk == 0
    head_spec = pl.BlockSpec((nb, A_HEADS, chunk, A_HEAD_DIM), lambda b, i: (b, 0, i, 0))
    par_spec = pl.BlockSpec((A_HEADS, 1, A_HEAD_DIM), lambda b, i: (0, 0, 0))
    per_head = lambda z: z.reshape(A_HEADS, 1, A_HEAD_DIM)
    return pl.pallas_call(
        functools.partial(_rwkv_scan_kernel, chunk=chunk),
        grid=(bsz // nb, t_len // chunk),
        in_specs=[head_spec] * 7 + [par_spec] * 5,
        out_specs=pl.BlockSpec((nb, chunk, A_WIDTH), lambda b, i: (b, i, 0)),
        out_shape=jax.ShapeDtypeStruct((bsz, t_len, A_WIDTH), F32),
        scratch_shapes=[pltpu.VMEM((nb * A_HEADS, A_HEAD_DIM, A_HEAD_DIM), F32)],
        compiler_params=_cparams(("parallel", "arbitrary")),
        name="rwkv_scan",
    )(r, k, v, lw, cs, a, g, per_head(k_k), per_head(k_a), per_head(r_k), per_head(ln_w),
      per_head(ln_b))


def _sb_attn_kernel(q_ref, k_ref, v_ref, o_ref, *, blk):
    i = pl.program_id(1)
    scale = B_HEAD_DIM ** -0.5
    lane = lax.broadcasted_iota(jnp.int32, (1, LANES), 1)
    head0 = lane < B_HEAD_DIM
    pairs = B_WIDTH // LANES
    zero_q = jnp.zeros((blk, LANES), BF16)
    qh = []
    for p in range(pairs):
        qp = q_ref[:, p * LANES:(p + 1) * LANES] * scale
        qh += [jnp.where(head0, qp, zero_q), jnp.where(head0, zero_q, qp)]

    ju = lax.broadcasted_iota(jnp.int32, (blk, blk + LANES), 0)
    su = lax.broadcasted_iota(jnp.int32, (blk, blk + LANES), 1)
    u = jnp.where((ju > su) | (su >= blk), 1.0, 0.0).astype(BF16)
    u2 = jnp.concatenate([u, u], axis=0)

    def block(kb, cs, accs, diagonal):
        start = pl.multiple_of(kb * blk, blk)
        if diagonal:
            mask = (lax.broadcasted_iota(jnp.int32, (blk, blk), 1)
                    < lax.broadcasted_iota(jnp.int32, (blk, blk), 0))
        heads = range(B_HEADS)
        cols = [pl.ds((h // 2) * LANES, LANES) for h in heads]
        z = [_dot_nt(qh[h], k_ref[pl.ds(start, blk), cols[h]]) for h in heads]
        lbeta, lu = [], []
        for h in heads:
            sp = jnp.maximum(z[h], 0.0) + jnp.log(1.0 + jnp.exp(-jnp.abs(z[h])))
            lbeta.append(z[h] - sp)
            skip = jnp.where(mask, sp, 0.0) if diagonal else sp
            s_hi = skip.astype(BF16)
            s_lo = (skip - s_hi.astype(F32)).astype(BF16)
            lu.append(_dot(jnp.concatenate([s_hi, s_lo], axis=1), u2))
        new_cs, new_accs = [], []
        for h in heads:
            amat = jnp.exp(lbeta[h] - lu[h][:, :blk] + cs[h])
            if diagonal:
                amat = jnp.where(mask, amat, 0.0)
            new_accs.append(accs[h] + _dot(amat.astype(BF16), v_ref[pl.ds(start, blk), cols[h]]))
            new_cs.append(cs[h] - lu[h][:, blk:])
        return new_cs, new_accs

    def cmax_of(cs):
        m = cs[0]
        for z in cs[1:]:
            m = jnp.maximum(m, z)
        return jnp.max(m)

    def cond(carry):
        return jnp.logical_and(carry[0] >= 0, carry[1] > EXP_ZERO_BELOW)

    def body(carry):
        kb = carry[0]
        cs, accs = block(kb, list(carry[2:2 + B_HEADS]), list(carry[2 + B_HEADS:]), diagonal=False)
        return (kb - 1, cmax_of(cs), *cs, *accs)

    zeros = [jnp.zeros((blk, LANES), F32)] * B_HEADS
    cs, accs = block(i, zeros, zeros, diagonal=True)
    out = lax.while_loop(cond, body, (i - 1, cmax_of(cs), *cs, *accs))
    accs = out[2 + B_HEADS:]
    for p in range(pairs):
        o_ref[:, p * LANES:(p + 1) * LANES] = jnp.where(head0, accs[2 * p], accs[2 * p + 1])


def stick_breaking_attention(proj, bsz, t_len, blk=128):
    assert blk == LANES
    nq = t_len // blk
    qc, kc, vc = (PB_OFF // B_WIDTH, PB_OFF // B_WIDTH + 1, PB_OFF // B_WIDTH + 2)
    return pl.pallas_call(
        functools.partial(_sb_attn_kernel, blk=blk),
        grid=(bsz, nq),
        in_specs=[
            pl.BlockSpec((blk, B_WIDTH), lambda b, i: (b * nq + i, qc)),
            pl.BlockSpec((t_len, B_WIDTH), lambda b, i: (b, kc)),
            pl.BlockSpec((t_len, B_WIDTH), lambda b, i: (b, vc)),
        ],
        out_specs=pl.BlockSpec((blk, B_WIDTH), lambda b, i: (b * nq + i, 0)),
        out_shape=jax.ShapeDtypeStruct((bsz * t_len, B_WIDTH), F32),
        compiler_params=_cparams(("parallel", "arbitrary")),
        name="stick_breaking_attention",
    )(proj, proj, proj)


def _rope_kernel(q_ref, k_ref, cos_ref, sinm_ref, sinp_ref, qo_ref, ko_ref):
    cos, sinm, sinp = cos_ref[...], sinm_ref[...], sinp_ref[...]
    half = C_ROT_DIMS // 2
    scale = C_QK_DIM ** -0.5

    def rot(x):
        return x * cos + pltpu.roll(x, LANES - half, axis=1) * sinm + pltpu.roll(x, half, axis=1) * sinp

    for h in range(C_HEADS):
        sl = slice(h * LANES, (h + 1) * LANES)
        qo_ref[:, sl] = (rot(q_ref[:, sl].astype(F32)) * scale).astype(BF16)
        ko_ref[:, sl] = rot(k_ref[:, sl].astype(F32)).astype(BF16)


def rope_qk(proj, bsz, t_len, cos_t, sinm_t, sinp_t, tm=512):
    n = bsz * t_len
    tpb = t_len // tm
    qc, kc = PC_OFF // C_QK_COLS, (PC_OFF + C_QK_COLS) // C_QK_COLS
    tab = pl.BlockSpec((tm, LANES), lambda i: (i % tpb, 0))
    out = jax.ShapeDtypeStruct((n, C_QK_COLS), BF16)
    return pl.pallas_call(
        _rope_kernel,
        grid=(n // tm,),
        in_specs=[
            pl.BlockSpec((tm, C_QK_COLS), lambda i: (i, qc)),
            pl.BlockSpec((tm, C_QK_COLS), lambda i: (i, kc)),
            tab, tab, tab,
        ],
        out_specs=[pl.BlockSpec((tm, C_QK_COLS), lambda i: (i, 0))] * 2,
        out_shape=[out, out],
        compiler_params=_cparams(("parallel",)),
        name="rope_qk",
    )(proj, proj, cos_t, sinm_t, sinp_t)


def _diff_attn_kernel(q_ref, k_ref, v_ref, lam_ref, g_ref, o_ref, *, blk, lambda_init):
    i = pl.program_id(1)
    lane = lax.broadcasted_iota(jnp.int32, (1, LANES), 1)
    half0 = lane < C_QK_DIM
    zero_q = jnp.zeros((blk, LANES), BF16)
    chains = [(h, s) for h in range(C_HEADS) for s in range(2)]
    qc = []
    for h in range(C_HEADS):
        q = q_ref[:, h * LANES:(h + 1) * LANES]
        qc += [jnp.where(half0, q, zero_q), jnp.where(half0, zero_q, q)]
    nc = len(chains)
    hcols = [pl.ds(h * LANES, LANES) for h, _ in chains]

    def scores(kb, width, diagonal=False):
        rows = pl.ds(pl.multiple_of(kb * blk, blk), width)
        z = [_dot_nt(qc[c], k_ref[rows, hcols[c]]) for c in range(nc)]
        if diagonal:
            mask = (lax.broadcasted_iota(jnp.int32, (blk, blk), 1) // CHUNK
                    <= lax.broadcasted_iota(jnp.int32, (blk, blk), 0) // CHUNK)
            z = [jnp.where(mask, zc, -jnp.inf) for zc in z]
        return z, rows

    def fold(z):
        out = z[:, :LANES]
        for s in range(1, z.shape[1] // LANES):
            out = jnp.maximum(out, z[:, s * LANES:(s + 1) * LANES])
        return out

    pairs = i // 2
    odd = i % 2

    def max_step(kb, width, rm):
        z, _ = scores(kb, width)
        return tuple(jnp.maximum(rm[c], fold(z[c])) for c in range(nc))

    rm = (jnp.full((blk, LANES), -jnp.inf, F32),) * nc
    rm = lax.fori_loop(0, pairs, lambda t, rm: max_step(2 * t, 2 * blk, rm), rm)
    rm = lax.fori_loop(0, odd, lambda t, rm: max_step(i - 1, blk, rm), rm)
    zd, rows_d = scores(i, blk, diagonal=True)
    m = [jnp.max(jnp.maximum(rm[c], fold(zd[c])), axis=-1, keepdims=True) for c in range(nc)]

    def accumulate(z, rows, accs):
        ps = [jnp.exp(z[c] - m[c]).astype(BF16) for c in range(nc)]
        ones = jnp.ones((z[0].shape[1], LANES), BF16)
        return tuple(accs[c] + _dot(ps[c], jnp.concatenate([v_ref[rows, hcols[c]], ones], axis=1))
                     for c in range(nc))

    def acc_step(kb, width, accs):
        z, rows = scores(kb, width)
        return accumulate(z, rows, accs)

    accs = (jnp.zeros((blk, 2 * LANES), F32),) * nc
    accs = lax.fori_loop(0, pairs, lambda t, accs: acc_step(2 * t, 2 * blk, accs), accs)
    accs = lax.fori_loop(0, odd, lambda t, accs: acc_step(i - 1, blk, accs), accs)
    accs = accumulate(zd, rows_d, accs)

    lv = lam_ref[...]
    lam = (jnp.exp(jnp.sum(lv[0:1] * lv[1:2], axis=-1, keepdims=True))
           - jnp.exp(jnp.sum(lv[2:3] * lv[3:4], axis=-1, keepdims=True)) + lambda_init)
    for h in range(C_HEADS):
        a1, a2 = accs[2 * h], accs[2 * h + 1]
        o = a1[:, :LANES] / a1[:, LANES:] - lam * (a2[:, :LANES] / a2[:, LANES:])
        o = o * lax.rsqrt(jnp.mean(o * o, axis=-1, keepdims=True) + EPS) * g_ref[...]
        o_ref[:, h * LANES:(h + 1) * LANES] = o * (1.0 - lambda_init)


def differential_attention(q_rot, k_rot, proj, bsz, t_len, lam_vecs, subln_g, lambda_init, blk=256):
    nq = t_len // blk
    vc = (PC_OFF + 2 * C_QK_COLS) // C_WIDTH
    return pl.pallas_call(
        functools.partial(_diff_attn_kernel, blk=blk, lambda_init=lambda_init),
        grid=(bsz, nq),
        in_specs=[
            pl.BlockSpec((blk, C_QK_COLS), lambda b, i: (b * nq + i, 0)),
            pl.BlockSpec((t_len, C_QK_COLS), lambda b, i: (b, 0)),
            pl.BlockSpec((t_len, C_WIDTH), lambda b, i: (b, vc)),
            pl.BlockSpec((4, C_QK_DIM), lambda b, i: (0, 0)),
            pl.BlockSpec((1, C_V_DIM), lambda b, i: (0, 0)),
        ],
        out_specs=pl.BlockSpec((blk, C_WIDTH), lambda b, i: (b * nq + i, 0)),
        out_shape=jax.ShapeDtypeStruct((bsz * t_len, C_WIDTH), F32),
        compiler_params=_cparams(("parallel", "arbitrary")),
        name="differential_attention",
    )(q_rot, k_rot, proj, lam_vecs, subln_g)


def _merge_kernel(x_ref, oa_ref, ob_ref, oc_ref, ga_ref, gb_ref, gc_ref, wa_ref, wb_ref, wc_ref,
                  wo_ref, o_ref):
    def branch(o_r, g_r, w_r):
        return jax.nn.sigmoid(g_r[...].astype(F32)) * _dot(o_r[...].astype(BF16), w_r[...])

    merged = (branch(oa_ref, ga_ref, wa_ref) + branch(ob_ref, gb_ref, wb_ref)
              + branch(oc_ref, gc_ref, wc_ref))
    o_ref[...] = x_ref[...] + _dot(merged.astype(BF16), wo_ref[...])


def merge_branches(x2, oa, ob, oc, proj, wa, wb, wc, wo, tm=512):
    n = x2.shape[0]
    g0 = PG_OFF // D_MODEL
    row = lambda width: pl.BlockSpec((tm, width), lambda i: (i, 0))
    gate = lambda j: pl.BlockSpec((tm, D_MODEL), lambda i: (i, g0 + j))
    wspec = lambda rows: pl.BlockSpec((rows, D_MODEL), lambda i: (0, 0))
    return pl.pallas_call(
        _merge_kernel,
        grid=(n // tm,),
        in_specs=[row(D_MODEL), row(A_WIDTH), row(B_WIDTH), row(C_WIDTH), gate(0), gate(1), gate(2),
                  wspec(A_WIDTH), wspec(B_WIDTH), wspec(C_WIDTH), wspec(D_MODEL)],
        out_specs=row(D_MODEL),
        out_shape=jax.ShapeDtypeStruct((n, D_MODEL), F32),
        compiler_params=_cparams(("parallel",)),
        name="merge_branches",
    )(x2, oa, ob, oc, proj, proj, proj, wa, wb, wc, wo)


def _ffn_kernel(x_ref, xp_ref, g_ref, wg_ref, wv_ref, cwg_ref, cwv_ref, cbg_ref, cbv_ref, wd_ref,
                gf_ref, o_ref, h_ref, acc_ref, *, final_norm, sub):
    tm = x_ref.shape[0]
    halo = FFN_HALO
    j = pl.program_id(2)

    def norm(x):
        return x * lax.rsqrt(jnp.mean(x * x, axis=-1, keepdims=True) + EPS)

    @pl.when(j == 0)
    def _():
        g = g_ref[...]
        h_ref[halo:, :] = (norm(x_ref[...]) * g).astype(BF16)
        hp = norm(xp_ref[...]) * g
        hp = jnp.where(pl.program_id(1) == 0, 0.0, hp).astype(BF16)
        h_ref[:halo, :] = jnp.concatenate([jnp.zeros_like(hp)] * (halo // SUBLANES - 1) + [hp], axis=0)
        acc_ref[...] = jnp.zeros_like(acc_ref)

    wg, wv, wd = wg_ref[...], wv_ref[...], wd_ref[...]

    def up(s):
        hs = h_ref[pl.ds(s * sub, sub + halo), :]
        return _dot(hs, wg), _dot(hs, wv)

    def conv(u, cw_ref, cb_ref):
        out = cb_ref[...] + cw_ref[CONV_WIDTH - 1:CONV_WIDTH, :] * u
        for back in range(1, CONV_WIDTH):
            tap = CONV_WIDTH - 1 - back
            out = out + cw_ref[tap:tap + 1, :] * pltpu.roll(u, back, axis=0)
        return out[halo:]

    def down(s, ug, uv):
        gate = conv(ug, cwg_ref, cbg_ref)
        val = conv(uv, cwv_ref, cbv_ref)
        act = (gate * jax.nn.sigmoid(gate) * val).astype(BF16)
        acc_ref[pl.ds(s * sub, sub), :] += _dot(act, wd)

    n_sub = tm // sub
    cur = up(0)
    for s in range(n_sub):
        nxt = up(s + 1) if s + 1 < n_sub else None
        down(s, *cur)
        cur = nxt

    @pl.when(j == pl.num_programs(2) - 1)
    def _():
        y = x_ref[...] + acc_ref[...]
        if final_norm:
            y = norm(y) * gf_ref[...]
        o_ref[...] = y


def conv_ffn(x2, bsz, t_len, g, w_up, conv_w, conv_b, w_down, g_final, final_norm, tm=1024, tf=256,
             sub=512):
    n = x2.shape[0]
    assert t_len % tm == 0 and tm % sub == 0 and D_FF % tf == 0
    tpb = t_len // tm
    nf = D_FF // tf
    rb = tm // SUBLANES

    def prev_map(b, i, j):
        return (jnp.maximum((b * tpb + i) * rb - 1, 0), 0)

    return pl.pallas_call(
        functools.partial(_ffn_kernel, final_norm=final_norm, sub=sub),
        grid=(bsz, tpb, nf),
        in_specs=[
            pl.BlockSpec((tm, D_MODEL), lambda b, i, j: (b * tpb + i, 0)),
            pl.BlockSpec((SUBLANES, D_MODEL), prev_map),
            pl.BlockSpec((1, D_MODEL), lambda b, i, j: (0, 0)),
            pl.BlockSpec((D_MODEL, tf), lambda b, i, j: (0, j)),
            pl.BlockSpec((D_MODEL, tf), lambda b, i, j: (0, nf + j)),
            pl.BlockSpec((CONV_WIDTH, tf), lambda b, i, j: (0, j)),
            pl.BlockSpec((CONV_WIDTH, tf), lambda b, i, j: (0, nf + j)),
            pl.BlockSpec((1, tf), lambda b, i, j: (0, j)),
            pl.BlockSpec((1, tf), lambda b, i, j: (0, nf + j)),
            pl.BlockSpec((tf, D_MODEL), lambda b, i, j: (j, 0)),
            pl.BlockSpec((1, D_MODEL), lambda b, i, j: (0, 0)),
        ],
        out_specs=pl.BlockSpec((tm, D_MODEL), lambda b, i, j: (b * tpb + i, 0)),
        out_shape=jax.ShapeDtypeStruct((n, D_MODEL), F32),
        scratch_shapes=[
            pltpu.VMEM((tm + FFN_HALO, D_MODEL), BF16),
            pltpu.VMEM((tm, D_MODEL), F32),
        ],
        compiler_params=_cparams(("parallel", "parallel", "arbitrary")),
        name="conv_ffn",
    )(x2, x2, g, w_up, w_up, conv_w, conv_w, conv_b, conv_b, w_down, g_final)


def _rope_tables(t_len):
    half = C_ROT_DIMS // 2
    inv_freq = ROPE_THETA ** (-jnp.arange(0, C_ROT_DIMS, 2, dtype=F32) / C_ROT_DIMS)
    ang = jnp.arange(t_len, dtype=F32)[:, None] * inv_freq[None, :]
    cos, sin = jnp.cos(ang), jnp.sin(ang)
    rest = C_QK_DIM - C_ROT_DIMS
    one = jnp.ones((t_len, rest), F32)
    zero = jnp.zeros((t_len, rest), F32)
    zh = jnp.zeros((t_len, half), F32)
    cos_t = jnp.concatenate([cos, cos, one], axis=1)
    sinm_t = jnp.concatenate([-sin, zh, zero], axis=1)
    sinp_t = jnp.concatenate([zh, sin, zero], axis=1)
    dup = lambda z: jnp.concatenate([z, z], axis=1)
    return dup(cos_t), dup(sinm_t), dup(sinp_t)


def _pad_w_in(w):
    pad = jnp.zeros((D_MODEL, PA_PAD - A_SHIFT_COLS), w.dtype)
    return jnp.concatenate([w[:, :A_SHIFT_COLS], pad, w[:, A_SHIFT_COLS:]], axis=1).astype(BF16)


def kernel(x, norm_mix_g, norm_ffn_g, w_in, rwkv_mu, rwkv_w0, rwkv_w_up, rwkv_a0, rwkv_a_up,
           rwkv_g_up, rwkv_k_k, rwkv_k_a, rwkv_r_k, rwkv_ln_w, rwkv_ln_b, diff_lambda, diff_subln_g,
           w_branch_a, w_branch_b, w_branch_c, w_out, ffn_w_up, ffn_conv_w, ffn_conv_b, ffn_w_down,
           norm_final_g):
    bsz, t_len, d_model = x.shape
    assert d_model == D_MODEL and t_len % ROW_TILE == 0, (x.shape, ROW_TILE)
    n = bsz * t_len
    cos_t, sinm_t, sinp_t = _rope_tables(t_len)
    row = lambda z: z.reshape(1, -1)
    x2 = x.reshape(n, D_MODEL)

    for l in range(DEPTH):
        proj = in_projection(x2, row(norm_mix_g[l]), _pad_w_in(w_in[l]))

        r, k, v, lw, cs, a, g = rwkv_prep(
            proj, bsz, t_len, row(rwkv_mu[l]), row(rwkv_w0[l]), rwkv_w_up[l].astype(BF16),
            row(rwkv_a0[l]), rwkv_a_up[l].astype(BF16), rwkv_g_up[l].astype(BF16), RWKV_CHUNK)
        oa = rwkv_scan(r, k, v, lw, cs, a, g, rwkv_k_k[l], rwkv_k_a[l], rwkv_r_k[l].reshape(-1),
                       rwkv_ln_w[l], rwkv_ln_b[l], RWKV_CHUNK).reshape(n, A_WIDTH)

        ob = stick_breaking_attention(proj, bsz, t_len)

        lambda_init = 0.8 - 0.6 * float(np.exp(-0.3 * l))
        q_rot, k_rot = rope_qk(proj, bsz, t_len, cos_t, sinm_t, sinp_t)
        oc = differential_attention(q_rot, k_rot, proj, bsz, t_len, diff_lambda[l],
                                    row(diff_subln_g[l]), lambda_init)

        x2 = merge_branches(x2, oa, ob, oc, proj, w_branch_a[l].astype(BF16),
                            w_branch_b[l].astype(BF16), w_branch_c[l].astype(BF16),
                            w_out[l].astype(BF16))

        x2 = conv_ffn(x2, bsz, t_len, row(norm_ffn_g[l]), ffn_w_up[l].astype(BF16), ffn_conv_w[l],
                      row(ffn_conv_b[l]), ffn_w_down[l].astype(BF16), row(norm_final_g),
                      final_norm=(l == DEPTH - 1))

    return x2.reshape(bsz, t_len, D_MODEL)
```

```python
import functools

import numpy as np
import jax
import jax.numpy as jnp
from jax import lax
from jax.experimental import pallas as pl
from jax.experimental.pallas import tpu as pltpu

F32 = jnp.float32
BF16 = jnp.bfloat16

D_MODEL = 1024
DEPTH = 4
CHUNK = 64
RWKV_CHUNK = 128
EPS = 1e-6

A_HEADS = 8
A_HEAD_DIM = 64
A_WIDTH = A_HEADS * A_HEAD_DIM
A_DECAY_LORA = 64
A_AAA_LORA = 64
A_GATE_LORA = 128
A_GN_EPS = 64e-5
A_DECAY_SCALE = 0.6065306597126334
A_SHIFT_COLS = 3 * A_WIDTH + A_DECAY_LORA + A_AAA_LORA + A_GATE_LORA

B_HEADS = 8
B_HEAD_DIM = 64
B_WIDTH = B_HEADS * B_HEAD_DIM

C_HEADS = 4
C_QK_DIM = 64
C_V_DIM = 2 * C_QK_DIM
C_QK_COLS = C_HEADS * 2 * C_QK_DIM
C_WIDTH = C_HEADS * C_V_DIM
ROPE_THETA = 500000.0
C_ROT_DIMS = C_QK_DIM // 4

D_FF = 2816
CONV_WIDTH = 3

LANES = 128
SUBLANES = 8

PA_OFF = 0
PA_PAD = 2048
PB_OFF = PA_PAD
PC_OFF = PB_OFF + 3 * B_WIDTH
PG_OFF = PC_OFF + 2 * C_QK_COLS + C_WIDTH
PROJ_COLS = PG_OFF + 3 * D_MODEL

VMEM_LIMIT = 56 * 1024 * 1024
ROW_TILE = 1024
FFN_HALO = 2 * SUBLANES

EXP_ZERO_BELOW = -104.0


def _cparams(sem):
    return pltpu.CompilerParams(dimension_semantics=sem, vmem_limit_bytes=VMEM_LIMIT)


def _dot(a, b):
    return jnp.dot(a, b, preferred_element_type=F32)


def _dot_nt(a, b):
    return lax.dot_general(a, b, (((1,), (1,)), ((), ())), preferred_element_type=F32)


def _inproj_kernel(x_ref, g_ref, w_ref, o_ref, h_ref):
    @pl.when(pl.program_id(1) == 0)
    def _():
        x = x_ref[...]
        ms = jnp.mean(x * x, axis=-1, keepdims=True)
        h_ref[...] = (x * lax.rsqrt(ms + EPS) * g_ref[...]).astype(BF16)

    o_ref[...] = _dot(h_ref[...], w_ref[...]).astype(o_ref.dtype)


def in_projection(x2, g, w_pad, tm=1024, tn=1024):
    n = x2.shape[0]
    return pl.pallas_call(
        _inproj_kernel,
        grid=(n // tm, PROJ_COLS // tn),
        in_specs=[
            pl.BlockSpec((tm, D_MODEL), lambda i, j: (i, 0)),
            pl.BlockSpec((1, D_MODEL), lambda i, j: (0, 0)),
            pl.BlockSpec((D_MODEL, tn), lambda i, j: (0, j)),
        ],
        out_specs=pl.BlockSpec((tm, tn), lambda i, j: (i, j)),
        out_shape=jax.ShapeDtypeStruct((n, PROJ_COLS), BF16),
        scratch_shapes=[pltpu.VMEM((tm, D_MODEL), BF16)],
        compiler_params=_cparams(("parallel", "arbitrary")),
        name="in_projection",
    )(x2, g, w_pad)


def _rwkv_prep_kernel(pa_ref, prev_ref, mu_ref, w0_ref, wup_ref, a0_ref, aup_ref, gup_ref,
                      r_ref, k_ref, v_ref, lw_ref, cs_ref, a_ref, g_ref, *, chunk):
    tm = pa_ref.shape[0]
    cur = pa_ref[:, :A_SHIFT_COLS].astype(F32)
    prev_rows = prev_ref.shape[0]
    prev_last = prev_ref[...].astype(F32)[prev_rows - 1:prev_rows, :A_SHIFT_COLS]
    prev_last = jnp.where(pl.program_id(1) == 0, 0.0, prev_last)
    shifted = pltpu.roll(cur, 1, axis=0)
    row = lax.broadcasted_iota(jnp.int32, (tm, 1), 0)
    shifted = jnp.where(row == 0, prev_last, shifted)
    xs = cur + (shifted - cur) * mu_ref[...]

    w = A_WIDTH
    r = xs[:, 0:w]
    k = xs[:, w:2 * w]
    v = xs[:, 2 * w:3 * w]
    dw = xs[:, 3 * w:3 * w + A_DECAY_LORA]
    da = xs[:, 3 * w + A_DECAY_LORA:3 * w + A_DECAY_LORA + A_AAA_LORA]
    dg = xs[:, 3 * w + A_DECAY_LORA + A_AAA_LORA:A_SHIFT_COLS]

    lw = -A_DECAY_SCALE * jax.nn.sigmoid(w0_ref[...] + _dot(jnp.tanh(dw).astype(BF16), wup_ref[...]))
    a = jax.nn.sigmoid(a0_ref[...] + _dot(da.astype(BF16), aup_ref[...]))
    g = _dot(jax.nn.sigmoid(dg).astype(BF16), gup_ref[...])

    ti = lax.broadcasted_iota(jnp.int32, (tm, tm), 0)
    si = lax.broadcasted_iota(jnp.int32, (tm, tm), 1)
    tri = jnp.where((ti // chunk == si // chunk) & (si <= ti), 1.0, 0.0).astype(BF16)
    lw_hi = lw.astype(BF16)
    lw_lo = (lw - lw_hi.astype(F32)).astype(BF16)
    cs = _dot(tri, lw_hi) + _dot(tri, lw_lo)

    for h in range(A_HEADS):
        sl = slice(h * A_HEAD_DIM, (h + 1) * A_HEAD_DIM)
        r_ref[0, h] = r[:, sl]
        k_ref[0, h] = k[:, sl]
        v_ref[0, h] = v[:, sl]
        lw_ref[0, h] = lw[:, sl]
        cs_ref[0, h] = cs[:, sl]
        a_ref[0, h] = a[:, sl]
        g_ref[0, h] = g[:, sl]


def rwkv_prep(proj, bsz, t_len, mu, w0, w_up, a0, a_up, g_up, chunk, tm=256):
    tpb = t_len // tm
    prev_rows = 2 * SUBLANES
    rb = tm // prev_rows

    def prev_map(b, i):
        return (jnp.maximum((b * tpb + i) * rb - 1, 0), 0)

    head_shape = jax.ShapeDtypeStruct((bsz, A_HEADS, t_len, A_HEAD_DIM), F32)
    head_spec = pl.BlockSpec((1, A_HEADS, tm, A_HEAD_DIM), lambda b, i: (b, 0, i, 0))
    full = lambda shape: pl.BlockSpec(shape, lambda b, i: (0,) * len(shape))
    return pl.pallas_call(
        functools.partial(_rwkv_prep_kernel, chunk=chunk),
        grid=(bsz, tpb),
        in_specs=[
            pl.BlockSpec((tm, PA_PAD), lambda b, i: (b * tpb + i, 0)),
            pl.BlockSpec((prev_rows, PA_PAD), prev_map),
            full((1, A_SHIFT_COLS)),
            full((1, A_WIDTH)),
            full((A_DECAY_LORA, A_WIDTH)),
            full((1, A_WIDTH)),
            full((A_AAA_LORA, A_WIDTH)),
            full((A_GATE_LORA, A_WIDTH)),
        ],
        out_specs=[head_spec] * 7,
        out_shape=[head_shape] * 7,
        compiler_params=_cparams(("parallel", "arbitrary")),
        name="rwkv_prep",
    )(proj, proj, mu, w0, w_up, a0, a_up, g_up)


def _rwkv_scan_kernel(r_ref, k_ref, v_ref, lw_ref, cs_ref, a_ref, g_ref, kk_ref, ka_ref, rk_ref,
                      lnw_ref, lnb_ref, o_ref, s_ref, *, chunk):
    c = chunk
    d = A_HEAD_DIM
    nb = r_ref.shape[0]
    units = [(b, h) for b in range(nb) for h in range(A_HEADS)]
    idx = range(len(units))
    bf = lambda z: z.astype(BF16)

    @pl.when(pl.program_id(1) == 0)
    def _():
        s_ref[...] = jnp.zeros_like(s_ref)

    ti = lax.broadcasted_iota(jnp.int32, (c, c), 0)
    si = lax.broadcasted_iota(jnp.int32, (c, c), 1)
    strict = si < ti
    incl = si <= ti
    n_double = int(np.ceil(np.log2(c)))

    r = [r_ref[b, h] for b, h in units]
    v = [v_ref[b, h] for b, h in units]
    vb = [bf(z) for z in v]
    kmod, at, bt, kt, rt, gend = [], [], [], [], [], []
    for u, (b, h) in enumerate(units):
        k = k_ref[b, h]
        a = a_ref[b, h]
        cs = cs_ref[b, h]
        kk = k * kk_ref[h]
        kk = kk * lax.rsqrt(jnp.maximum(jnp.sum(kk * kk, axis=-1, keepdims=True), 1e-24))
        kmod.append(k * (1.0 + (a - 1.0) * ka_ref[h]))
        gam = jnp.exp(cs)
        ginv = jnp.exp(-cs)
        at.append(-kk * jnp.exp(cs - lw_ref[b, h]))
        bt.append(kk * a * ginv)
        kt.append(kmod[u] * ginv)
        rt.append(r[u] * gam)
        gend.append(gam[c - 1:c, :])

    m_ab, p_rb, m_ak, p_rk = [], [], [], []
    for u in idx:
        lhs = bf(jnp.concatenate([at[u], rt[u]], axis=0))
        fb = _dot_nt(lhs, bf(bt[u]))
        fk = _dot_nt(lhs, bf(kt[u]))
        m_ab.append(jnp.where(strict, fb[:c], 0.0))
        p_rb.append(bf(jnp.where(incl, fb[c:], 0.0)))
        m_ak.append(bf(jnp.where(strict, fk[:c], 0.0)))
        p_rk.append(bf(jnp.where(incl, fk[c:], 0.0)))

    x = [jnp.concatenate([at[u], _dot(m_ak[u], vb[u])], axis=1) for u in idx]
    p = m_ab
    for j in range(n_double):
        pb = [bf(z) for z in p]
        x = [x[u] + _dot(pb[u], bf(x[u])) for u in idx]
        if j + 1 < n_double:
            p = [_dot(pb[u], pb[u]) for u in idx]

    outs = []
    for u, (b, h) in enumerate(units):
        s0 = s_ref[u]
        s0b = bf(s0)
        xh = bf(x[u])
        y = _dot(p_rb[u], xh)
        o = _dot_nt(bf(y[:, :d] + rt[u]), s0b) + y[:, d:] + _dot(p_rk[u], vb[u])

        xb = _dot(bf(x[u].T), bf(bt[u] * gend[u]))
        vk = _dot(bf(v[u].T), bf(kt[u] * gend[u]))
        s_ref[u] = s0 * gend[u] + _dot(s0b, bf(xb[:d])) + xb[d:] + vk

        mean = jnp.mean(o, axis=-1, keepdims=True)
        var = jnp.mean(jnp.square(o - mean), axis=-1, keepdims=True)
        o = (o - mean) * lax.rsqrt(var + A_GN_EPS)
        o = o * lnw_ref[h] + lnb_ref[h]
        bonus = jnp.sum(r[u] * kmod[u] * rk_ref[h], axis=-1, keepdims=True) * v[u]
        outs.append((o + bonus) * g_ref[b, h])

    for b in range(nb):
        o_ref[b] = jnp.concatenate(outs[b * A_HEADS:(b + 1) * A_HEADS], axis=1)


def rwkv_scan(r, k, v, lw, cs, a, g, k_k, k_a, r_k, ln_w, ln_b, chunk, nb=4):
    bsz, _, t_len, _ = r.shape
    assert bsz % nb == 0 and t_len % chunk == 0
    head_spec = pl.BlockSpec((nb, A_HEADS, chunk, A_HEAD_DIM), lambda b, i: (b, 0, i, 0))
    par_spec = pl.BlockSpec((A_HEADS, 1, A_HEAD_DIM), lambda b, i: (0, 0, 0))
    per_head = lambda z: z.reshape(A_HEADS, 1, A_HEAD_DIM)
    return pl.pallas_call(
        functools.partial(_rwkv_scan_kernel, chunk=chunk),
        grid=(bsz // nb, t_len // chunk),
        in_specs=[head_spec] * 7 + [par_spec] * 5,
        out_specs=pl.BlockSpec((nb, chunk, A_WIDTH), lambda b, i: (b, i, 0)),
        out_shape=jax.ShapeDtypeStruct((bsz, t_len, A_WIDTH), F32),
        scratch_shapes=[pltpu.VMEM((nb * A_HEADS, A_HEAD_DIM, A_HEAD_DIM), F32)],
        compiler_params=_cparams(("parallel", "arbitrary")),
        name="rwkv_scan",
    )(r, k, v, lw, cs, a, g, per_head(k_k), per_head(k_a), per_head(r_k), per_head(ln_w),
      per_head(ln_b))


def _sb_attn_kernel(q_ref, k_ref, v_ref, o_ref, *, blk):
    i = pl.program_id(1)
    scale = B_HEAD_DIM ** -0.5
    lane = lax.broadcasted_iota(jnp.int32, (1, LANES), 1)
    head0 = lane < B_HEAD_DIM
    pairs = B_WIDTH // LANES
    zero_q = jnp.zeros((blk, LANES), BF16)
    qh = []
    for p in range(pairs):
        qp = q_ref[:, p * LANES:(p + 1) * LANES] * scale
        qh += [jnp.where(head0, qp, zero_q), jnp.where(head0, zero_q, qp)]

    ju = lax.broadcasted_iota(jnp.int32, (blk, blk + LANES), 0)
    su = lax.broadcasted_iota(jnp.int32, (blk, blk + LANES), 1)
    u = jnp.where((ju > su) | (su >= blk), 1.0, 0.0).astype(BF16)
    u2 = jnp.concatenate([u, u], axis=0)

    def block(kb, cs, accs, diagonal):
        start = pl.multiple_of(kb * blk, blk)
        if diagonal:
            mask = (lax.broadcasted_iota(jnp.int32, (blk, blk), 1)
                    < lax.broadcasted_iota(jnp.int32, (blk, blk), 0))
        heads = range(B_HEADS)
        cols = [pl.ds((h // 2) * LANES, LANES) for h in heads]
        z = [_dot_nt(qh[h], k_ref[pl.ds(start, blk), cols[h]]) for h in heads]
        lbeta, lu = [], []
        for h in heads:
            sp = jnp.maximum(z[h], 0.0) + jnp.log(1.0 + jnp.exp(-jnp.abs(z[h])))
            lbeta.append(z[h] - sp)
            skip = jnp.where(mask, sp, 0.0) if diagonal else sp
            s_hi = skip.astype(BF16)
            s_lo = (skip - s_hi.astype(F32)).astype(BF16)
            lu.append(_dot(jnp.concatenate([s_hi, s_lo], axis=1), u2))
        new_cs, new_accs = [], []
        for h in heads:
            amat = jnp.exp(lbeta[h] - lu[h][:, :blk] + cs[h])
            if diagonal:
                amat = jnp.where(mask, amat, 0.0)
            new_accs.append(accs[h] + _dot(amat.astype(BF16), v_ref[pl.ds(start, blk), cols[h]]))
            new_cs.append(cs[h] - lu[h][:, blk:])
        return new_cs, new_accs

    def cmax_of(cs):
        m = cs[0]
        for z in cs[1:]:
            m = jnp.maximum(m, z)
        return jnp.max(m)

    def cond(carry):
        return jnp.logical_and(carry[0] >= 0, carry[1] > EXP_ZERO_BELOW)

    def body(carry):
        kb = carry[0]
        cs, accs = block(kb, list(carry[2:2 + B_HEADS]), list(carry[2 + B_HEADS:]), diagonal=False)
        return (kb - 1, cmax_of(cs), *cs, *accs)

    zeros = [jnp.zeros((blk, LANES), F32)] * B_HEADS
    cs, accs = block(i, zeros, zeros, diagonal=True)
    out = lax.while_loop(cond, body, (i - 1, cmax_of(cs), *cs, *accs))
    accs = out[2 + B_HEADS:]
    for p in range(pairs):
        o_ref[:, p * LANES:(p + 1) * LANES] = jnp.where(head0, accs[2 * p], accs[2 * p + 1])


def stick_breaking_attention(proj, bsz, t_len, blk=128):
    assert blk == LANES
    nq = t_len // blk
    qc, kc, vc = (PB_OFF // B_WIDTH, PB_OFF // B_WIDTH + 1, PB_OFF // B_WIDTH + 2)
    return pl.pallas_call(
        functools.partial(_sb_attn_kernel, blk=blk),
        grid=(bsz, nq),
        in_specs=[
            pl.BlockSpec((blk, B_WIDTH), lambda b, i: (b * nq + i, qc)),
            pl.BlockSpec((t_len, B_WIDTH), lambda b, i: (b, kc)),
            pl.BlockSpec((t_len, B_WIDTH), lambda b, i: (b, vc)),
        ],
        out_specs=pl.BlockSpec((blk, B_WIDTH), lambda b, i: (b * nq + i, 0)),
        out_shape=jax.ShapeDtypeStruct((bsz * t_len, B_WIDTH), F32),
        compiler_params=_cparams(("parallel", "arbitrary")),
        name="stick_breaking_attention",
    )(proj, proj, proj)


def _rope_kernel(q_ref, k_ref, cos_ref, sinm_ref, sinp_ref, qo_ref, ko_ref):
    cos, sinm, sinp = cos_ref[...], sinm_ref[...], sinp_ref[...]
    half = C_ROT_DIMS // 2
    scale = C_QK_DIM ** -0.5

    def rot(x):
        return x * cos + pltpu.roll(x, LANES - half, axis=1) * sinm + pltpu.roll(x, half, axis=1) * sinp

    for h in range(C_HEADS):
        sl = slice(h * LANES, (h + 1) * LANES)
        qo_ref[:, sl] = (rot(q_ref[:, sl].astype(F32)) * scale).astype(BF16)
        ko_ref[:, sl] = rot(k_ref[:, sl].astype(F32)).astype(BF16)


def rope_qk(proj, bsz, t_len, cos_t, sinm_t, sinp_t, tm=512):
    n = bsz * t_len
    tpb = t_len // tm
    qc, kc = PC_OFF // C_QK_COLS, (PC_OFF + C_QK_COLS) // C_QK_COLS
    tab = pl.BlockSpec((tm, LANES), lambda i: (i % tpb, 0))
    out = jax.ShapeDtypeStruct((n, C_QK_COLS), BF16)
    return pl.pallas_call(
        _rope_kernel,
        grid=(n // tm,),
        in_specs=[
            pl.BlockSpec((tm, C_QK_COLS), lambda i: (i, qc)),
            pl.BlockSpec((tm, C_QK_COLS), lambda i: (i, kc)),
            tab, tab, tab,
        ],
        out_specs=[pl.BlockSpec((tm, C_QK_COLS), lambda i: (i, 0))] * 2,
        out_shape=[out, out],
        compiler_params=_cparams(("parallel",)),
        name="rope_qk",
    )(proj, proj, cos_t, sinm_t, sinp_t)


def _diff_attn_kernel(q_ref, k_ref, v_ref, lam_ref, g_ref, o_ref, z_ref, *, blk, lambda_init):
    i = pl.program_id(1)
    lane = lax.broadcasted_iota(jnp.int32, (1, LANES), 1)
    half0 = lane < C_QK_DIM
    zero_q = jnp.zeros((blk, LANES), BF16)
    chains = [(h, s) for h in range(C_HEADS) for s in range(2)]
    qc = []
    for h in range(C_HEADS):
        q = q_ref[:, h * LANES:(h + 1) * LANES]
        qc += [jnp.where(half0, q, zero_q), jnp.where(half0, zero_q, q)]
    nc = len(chains)
    hcols = [pl.ds(h * LANES, LANES) for h, _ in chains]


    def fold(z):
        out = z[:, :LANES]
        for s in range(1, z.shape[1] // LANES):
            out = jnp.maximum(out, z[:, s * LANES:(s + 1) * LANES])
        return out

    def scores(kb, width, diagonal=False):
        rows = pl.ds(pl.multiple_of(kb * blk, blk), width)
        z = [_dot_nt(qc[c], k_ref[rows, hcols[c]]) for c in range(nc)]
        if diagonal:
            mask = (lax.broadcasted_iota(jnp.int32, (blk, blk), 1) // CHUNK
                    <= lax.broadcasted_iota(jnp.int32, (blk, blk), 0) // CHUNK)
            z = [jnp.where(mask, zc, -jnp.inf) for zc in z]
        return z, rows

    pairs = i // 2
    odd = i % 2

    def max_step(kb, width, rm):
        z, _ = scores(kb, width)
        for c in range(nc):
            for s in range(width // blk):
                z_ref[kb + s, c] = z[c][:, s * blk:(s + 1) * blk].astype(BF16)
        return tuple(jnp.maximum(rm[c], fold(z[c])) for c in range(nc))

    rm = (jnp.full((blk, LANES), -jnp.inf, F32),) * nc
    rm = lax.fori_loop(0, pairs, lambda t, rm: max_step(2 * t, 2 * blk, rm), rm)
    rm = lax.fori_loop(0, odd, lambda t, rm: max_step(i - 1, blk, rm), rm)
    zd, rows_d = scores(i, blk, diagonal=True)
    m = [jnp.max(jnp.maximum(rm[c], fold(zd[c])), axis=-1, keepdims=True) for c in range(nc)]

    def accumulate(z, rows, accs):
        ps = [jnp.exp(z[c] - m[c].astype(z[c].dtype)).astype(BF16) for c in range(nc)]
        ones = jnp.ones((z[0].shape[1], LANES), BF16)
        return tuple(accs[c] + _dot(ps[c], jnp.concatenate([v_ref[rows, hcols[c]], ones], axis=1))
                     for c in range(nc))

    def acc_step(kb, width, accs):
        rows = pl.ds(pl.multiple_of(kb * blk, blk), width)
        z = [jnp.concatenate([z_ref[kb + s, c] for s in range(width // blk)], axis=1)
             for c in range(nc)]
        return accumulate(z, rows, accs)

    accs = (jnp.zeros((blk, 2 * LANES), F32),) * nc
    accs = lax.fori_loop(0, pairs, lambda t, accs: acc_step(2 * t, 2 * blk, accs), accs)
    accs = lax.fori_loop(0, odd, lambda t, accs: acc_step(i - 1, blk, accs), accs)
    accs = accumulate(zd, rows_d, accs)

    lv = lam_ref[...]
    lam = (jnp.exp(jnp.sum(lv[0:1] * lv[1:2], axis=-1, keepdims=True))
           - jnp.exp(jnp.sum(lv[2:3] * lv[3:4], axis=-1, keepdims=True)) + lambda_init)
    for h in range(C_HEADS):
        a1, a2 = accs[2 * h], accs[2 * h + 1]
        o = a1[:, :LANES] / a1[:, LANES:] - lam * (a2[:, :LANES] / a2[:, LANES:])
        o = o * lax.rsqrt(jnp.mean(o * o, axis=-1, keepdims=True) + EPS) * g_ref[...]
        o_ref[:, h * LANES:(h + 1) * LANES] = o * (1.0 - lambda_init)


def differential_attention(q_rot, k_rot, proj, bsz, t_len, lam_vecs, subln_g, lambda_init, blk=256):
    nq = t_len // blk
    vc = (PC_OFF + 2 * C_QK_COLS) // C_WIDTH
    return pl.pallas_call(
        functools.partial(_diff_attn_kernel, blk=blk, lambda_init=lambda_init),
        grid=(bsz, nq),
        in_specs=[
            pl.BlockSpec((blk, C_QK_COLS), lambda b, i: (b * nq + i, 0)),
            pl.BlockSpec((t_len, C_QK_COLS), lambda b, i: (b, 0)),
            pl.BlockSpec((t_len, C_WIDTH), lambda b, i: (b, vc)),
            pl.BlockSpec((4, C_QK_DIM), lambda b, i: (0, 0)),
            pl.BlockSpec((1, C_V_DIM), lambda b, i: (0, 0)),
        ],
        out_specs=pl.BlockSpec((blk, C_WIDTH), lambda b, i: (b * nq + i, 0)),
        out_shape=jax.ShapeDtypeStruct((bsz * t_len, C_WIDTH), F32),
        scratch_shapes=[pltpu.VMEM((max(nq - 1, 1), 2 * C_HEADS, blk, blk), BF16)],
        compiler_params=_cparams(("parallel", "arbitrary")),
        name="differential_attention",
    )(q_rot, k_rot, proj, lam_vecs, subln_g)


def _merge_kernel(x_ref, oa_ref, ob_ref, oc_ref, ga_ref, gb_ref, gc_ref, wa_ref, wb_ref, wc_ref,
                  wo_ref, o_ref):
    def branch(o_r, g_r, w_r):
        return jax.nn.sigmoid(g_r[...].astype(F32)) * _dot(o_r[...].astype(BF16), w_r[...])

    merged = (branch(oa_ref, ga_ref, wa_ref) + branch(ob_ref, gb_ref, wb_ref)
              + branch(oc_ref, gc_ref, wc_ref))
    o_ref[...] = x_ref[...] + _dot(merged.astype(BF16), wo_ref[...])


def merge_branches(x2, oa, ob, oc, proj, wa, wb, wc, wo, tm=512):
    n = x2.shape[0]
    g0 = PG_OFF // D_MODEL
    row = lambda width: pl.BlockSpec((tm, width), lambda i: (i, 0))
    gate = lambda j: pl.BlockSpec((tm, D_MODEL), lambda i: (i, g0 + j))
    wspec = lambda rows: pl.BlockSpec((rows, D_MODEL), lambda i: (0, 0))
    return pl.pallas_call(
        _merge_kernel,
        grid=(n // tm,),
        in_specs=[row(D_MODEL), row(A_WIDTH), row(B_WIDTH), row(C_WIDTH), gate(0), gate(1), gate(2),
                  wspec(A_WIDTH), wspec(B_WIDTH), wspec(C_WIDTH), wspec(D_MODEL)],
        out_specs=row(D_MODEL),
        out_shape=jax.ShapeDtypeStruct((n, D_MODEL), F32),
        compiler_params=_cparams(("parallel",)),
        name="merge_branches",
    )(x2, oa, ob, oc, proj, proj, proj, wa, wb, wc, wo)


def _ffn_kernel(x_ref, xp_ref, g_ref, wg_ref, wv_ref, cwg_ref, cwv_ref, cbg_ref, cbv_ref, wd_ref,
                gf_ref, o_ref, h_ref, acc_ref, *, final_norm, sub):
    tm = x_ref.shape[0]
    halo = FFN_HALO
    j = pl.program_id(2)

    def norm(x):
        return x * lax.rsqrt(jnp.mean(x * x, axis=-1, keepdims=True) + EPS)

    @pl.when(j == 0)
    def _():
        g = g_ref[...]
        h_ref[halo:, :] = (norm(x_ref[...]) * g).astype(BF16)
        hp = norm(xp_ref[...]) * g
        hp = jnp.where(pl.program_id(1) == 0, 0.0, hp).astype(BF16)
        h_ref[:halo, :] = jnp.concatenate([jnp.zeros_like(hp)] * (halo // SUBLANES - 1) + [hp], axis=0)
        acc_ref[...] = jnp.zeros_like(acc_ref)

    wg, wv, wd = wg_ref[...], wv_ref[...], wd_ref[...]

    def up(s):
        hs = h_ref[pl.ds(s * sub, sub + halo), :]
        return _dot(hs, wg), _dot(hs, wv)

    def conv(u, cw_ref, cb_ref):
        out = cb_ref[...] + cw_ref[CONV_WIDTH - 1:CONV_WIDTH, :] * u
        for back in range(1, CONV_WIDTH):
            tap = CONV_WIDTH - 1 - back
            out = out + cw_ref[tap:tap + 1, :] * pltpu.roll(u, back, axis=0)
        return out[halo:]

    def down(s, ug, uv):
        gate = conv(ug, cwg_ref, cbg_ref)
        val = conv(uv, cwv_ref, cbv_ref)
        act = (gate * jax.nn.sigmoid(gate) * val).astype(BF16)
        acc_ref[pl.ds(s * sub, sub), :] += _dot(act, wd)

    n_sub = tm // sub
    cur = up(0)
    for s in range(n_sub):
        nxt = up(s + 1) if s + 1 < n_sub else None
        down(s, *cur)
        cur = nxt

    @pl.when(j == pl.num_programs(2) - 1)
    def _():
        y = x_ref[...] + acc_ref[...]
        if final_norm:
            y = norm(y) * gf_ref[...]
        o_ref[...] = y


def conv_ffn(x2, bsz, t_len, g, w_up, conv_w, conv_b, w_down, g_final, final_norm, tm=1024, tf=256,
             sub=512):
    n = x2.shape[0]
    assert t_len % tm == 0 and tm % sub == 0 and D_FF % tf == 0
    tpb = t_len // tm
    nf = D_FF // tf
    rb = tm // SUBLANES

    def prev_map(b, i, j):
        return (jnp.maximum((b * tpb + i) * rb - 1, 0), 0)

    return pl.pallas_call(
        functools.partial(_ffn_kernel, final_norm=final_norm, sub=sub),
        grid=(bsz, tpb, nf),
        in_specs=[
            pl.BlockSpec((tm, D_MODEL), lambda b, i, j: (b * tpb + i, 0)),
            pl.BlockSpec((SUBLANES, D_MODEL), prev_map),
            pl.BlockSpec((1, D_MODEL), lambda b, i, j: (0, 0)),
            pl.BlockSpec((D_MODEL, tf), lambda b, i, j: (0, j)),
            pl.BlockSpec((D_MODEL, tf), lambda b, i, j: (0, nf + j)),
            pl.BlockSpec((CONV_WIDTH, tf), lambda b, i, j: (0, j)),
            pl.BlockSpec((CONV_WIDTH, tf), lambda b, i, j: (0, nf + j)),
            pl.BlockSpec((1, tf), lambda b, i, j: (0, j)),
            pl.BlockSpec((1, tf), lambda b, i, j: (0, nf + j)),
            pl.BlockSpec((tf, D_MODEL), lambda b, i, j: (j, 0)),
            pl.BlockSpec((1, D_MODEL), lambda b, i, j: (0, 0)),
        ],
        out_specs=pl.BlockSpec((tm, D_MODEL), lambda b, i, j: (b * tpb + i, 0)),
        out_shape=jax.ShapeDtypeStruct((n, D_MODEL), F32),
        scratch_shapes=[
            pltpu.VMEM((tm + FFN_HALO, D_MODEL), BF16),
            pltpu.VMEM((tm, D_MODEL), F32),
        ],
        compiler_params=_cparams(("parallel", "parallel", "arbitrary")),
        name="conv_ffn",
    )(x2, x2, g, w_up, w_up, conv_w, conv_w, conv_b, conv_b, w_down, g_final)


def _rope_tables(t_len):
    half = C_ROT_DIMS // 2
    inv_freq = ROPE_THETA ** (-jnp.arange(0, C_ROT_DIMS, 2, dtype=F32) / C_ROT_DIMS)
    ang = jnp.arange(t_len, dtype=F32)[:, None] * inv_freq[None, :]
    cos, sin = jnp.cos(ang), jnp.sin(ang)
    rest = C_QK_DIM - C_ROT_DIMS
    one = jnp.ones((t_len, rest), F32)
    zero = jnp.zeros((t_len, rest), F32)
    zh = jnp.zeros((t_len, half), F32)
    cos_t = jnp.concatenate([cos, cos, one], axis=1)
    sinm_t = jnp.concatenate([-sin, zh, zero], axis=1)
    sinp_t = jnp.concatenate([zh, sin, zero], axis=1)
    dup = lambda z: jnp.concatenate([z, z], axis=1)
    return dup(cos_t), dup(sinm_t), dup(sinp_t)


def _pad_w_in(w):
    pad = jnp.zeros((D_MODEL, PA_PAD - A_SHIFT_COLS), w.dtype)
    return jnp.concatenate([w[:, :A_SHIFT_COLS], pad, w[:, A_SHIFT_COLS:]], axis=1).astype(BF16)


def kernel(x, norm_mix_g, norm_ffn_g, w_in, rwkv_mu, rwkv_w0, rwkv_w_up, rwkv_a0, rwkv_a_up,
           rwkv_g_up, rwkv_k_k, rwkv_k_a, rwkv_r_k, rwkv_ln_w, rwkv_ln_b, diff_lambda, diff_subln_g,
           w_branch_a, w_branch_b, w_branch_c, w_out, ffn_w_up, ffn_conv_w, ffn_conv_b, ffn_w_down,
           norm_final_g):
    bsz, t_len, d_model = x.shape
    assert d_model == D_MODEL and t_len % ROW_TILE == 0, (x.shape, ROW_TILE)
    n = bsz * t_len
    cos_t, sinm_t, sinp_t = _rope_tables(t_len)
    row = lambda z: z.reshape(1, -1)
    x2 = x.reshape(n, D_MODEL)

    for l in range(DEPTH):
        proj = in_projection(x2, row(norm_mix_g[l]), _pad_w_in(w_in[l]))

        r, k, v, lw, cs, a, g = rwkv_prep(
            proj, bsz, t_len, row(rwkv_mu[l]), row(rwkv_w0[l]), rwkv_w_up[l].astype(BF16),
            row(rwkv_a0[l]), rwkv_a_up[l].astype(BF16), rwkv_g_up[l].astype(BF16), RWKV_CHUNK)
        oa = rwkv_scan(r, k, v, lw, cs, a, g, rwkv_k_k[l], rwkv_k_a[l], rwkv_r_k[l].reshape(-1),
                       rwkv_ln_w[l], rwkv_ln_b[l], RWKV_CHUNK).reshape(n, A_WIDTH)

        ob = stick_breaking_attention(proj, bsz, t_len)

        lambda_init = 0.8 - 0.6 * float(np.exp(-0.3 * l))
        q_rot, k_rot = rope_qk(proj, bsz, t_len, cos_t, sinm_t, sinp_t)
        oc = differential_attention(q_rot, k_rot, proj, bsz, t_len, diff_lambda[l],
                                    row(diff_subln_g[l]), lambda_init)

        x2 = merge_branches(x2, oa, ob, oc, proj, w_branch_a[l].astype(BF16),
                            w_branch_b[l].astype(BF16), w_branch_c[l].astype(BF16),
                            w_out[l].astype(BF16))

        x2 = conv_ffn(x2, bsz, t_len, row(norm_ffn_g[l]), ffn_w_up[l].astype(BF16), ffn_conv_w[l],
                      row(ffn_conv_b[l]), ffn_w_down[l].astype(BF16), row(norm_final_g),
                      final_norm=(l == DEPTH - 1))

    return x2.reshape(bsz, t_len, D_MODEL)
```

```python
import functools

import numpy as np
import jax
import jax.numpy as jnp
from jax import lax
from jax.experimental import pallas as pl
from jax.experimental.pallas import tpu as pltpu

F32 = jnp.float32
BF16 = jnp.bfloat16

D_MODEL = 1024
DEPTH = 4
CHUNK = 64
RWKV_CHUNK = 128
EPS = 1e-6

A_HEADS = 8
A_HEAD_DIM = 64
A_WIDTH = A_HEADS * A_HEAD_DIM
A_DECAY_LORA = 64
A_AAA_LORA = 64
A_GATE_LORA = 128
A_GN_EPS = 64e-5
A_DECAY_SCALE = 0.6065306597126334
A_SHIFT_COLS = 3 * A_WIDTH + A_DECAY_LORA + A_AAA_LORA + A_GATE_LORA

B_HEADS = 8
B_HEAD_DIM = 64
B_WIDTH = B_HEADS * B_HEAD_DIM

C_HEADS = 4
C_QK_DIM = 64
C_V_DIM = 2 * C_QK_DIM
C_QK_COLS = C_HEADS * 2 * C_QK_DIM
C_WIDTH = C_HEADS * C_V_DIM
ROPE_THETA = 500000.0
C_ROT_DIMS = C_QK_DIM // 4

D_FF = 2816
CONV_WIDTH = 3

LANES = 128
SUBLANES = 8

PA_OFF = 0
PA_PAD = 2048
PB_OFF = PA_PAD
PC_OFF = PB_OFF + 3 * B_WIDTH
PG_OFF = PC_OFF + 2 * C_QK_COLS + C_WIDTH
PROJ_COLS = PG_OFF + 3 * D_MODEL

VMEM_LIMIT = 56 * 1024 * 1024
ROW_TILE = 1024
FFN_HALO = 2 * SUBLANES

EXP_ZERO_BELOW = -104.0


def _cparams(sem):
    return pltpu.CompilerParams(dimension_semantics=sem, vmem_limit_bytes=VMEM_LIMIT)


def _dot(a, b):
    return jnp.dot(a, b, preferred_element_type=F32)


def _dot_nt(a, b):
    return lax.dot_general(a, b, (((1,), (1,)), ((), ())), preferred_element_type=F32)


def _inproj_kernel(x_ref, g_ref, w_ref, cos_ref, sinm_ref, sinp_ref, o_ref, *, tn):
    x = x_ref[...]
    ms = jnp.mean(x * x, axis=-1, keepdims=True)
    h = (x * lax.rsqrt(ms + EPS) * g_ref[...]).astype(BF16)
    half = C_ROT_DIMS // 2
    q_scale = C_QK_DIM ** -0.5

    def rot(z):
        return (z * cos_ref[...] + pltpu.roll(z, LANES - half, axis=1) * sinm_ref[...]
                + pltpu.roll(z, half, axis=1) * sinp_ref[...])

    for j in range(PROJ_COLS // tn):
        acc = _dot(h, w_ref[:, j * tn:(j + 1) * tn])
        for s in range(tn // LANES):
            col = j * tn + s * LANES
            blk = acc[:, s * LANES:(s + 1) * LANES]
            if PC_OFF <= col < PC_OFF + C_QK_COLS:
                blk = rot(blk) * q_scale
            elif PC_OFF + C_QK_COLS <= col < PC_OFF + 2 * C_QK_COLS:
                blk = rot(blk)
            o_ref[:, col:col + LANES] = blk.astype(o_ref.dtype)


def in_projection(x2, g, w_pad, t_len, cos_t, sinm_t, sinp_t, tm=512, tn=1024):
    n = x2.shape[0]
    assert t_len % tm == 0 and PROJ_COLS % tn == 0
    tpb = t_len // tm
    resident = lambda shape: pl.BlockSpec(shape, lambda i: (0, 0), pipeline_mode=pl.Buffered(1))
    tab = pl.BlockSpec((tm, LANES), lambda i: (i % tpb, 0))
    return pl.pallas_call(
        functools.partial(_inproj_kernel, tn=tn),
        grid=(n // tm,),
        in_specs=[
            pl.BlockSpec((tm, D_MODEL), lambda i: (i, 0)),
            resident((1, D_MODEL)),
            resident((D_MODEL, PROJ_COLS)),
            tab, tab, tab,
        ],
        out_specs=pl.BlockSpec((tm, PROJ_COLS), lambda i: (i, 0)),
        out_shape=jax.ShapeDtypeStruct((n, PROJ_COLS), BF16),
        compiler_params=_cparams(("parallel",)),
        name="in_projection",
    )(x2, g, w_pad, cos_t, sinm_t, sinp_t)


def _rwkv_prep_kernel(pa_ref, prev_ref, mu_ref, w0_ref, wup_ref, a0_ref, aup_ref, gup_ref,
                      r_ref, k_ref, v_ref, lw_ref, cs_ref, a_ref, g_ref, *, chunk):
    tm = pa_ref.shape[0]
    cur = pa_ref[:, :A_SHIFT_COLS].astype(F32)
    prev_rows = prev_ref.shape[0]
    prev_last = prev_ref[...].astype(F32)[prev_rows - 1:prev_rows, :A_SHIFT_COLS]
    prev_last = jnp.where(pl.program_id(1) == 0, 0.0, prev_last)
    shifted = pltpu.roll(cur, 1, axis=0)
    row = lax.broadcasted_iota(jnp.int32, (tm, 1), 0)
    shifted = jnp.where(row == 0, prev_last, shifted)
    xs = cur + (shifted - cur) * mu_ref[...]

    w = A_WIDTH
    r = xs[:, 0:w]
    k = xs[:, w:2 * w]
    v = xs[:, 2 * w:3 * w]
    dw = xs[:, 3 * w:3 * w + A_DECAY_LORA]
    da = xs[:, 3 * w + A_DECAY_LORA:3 * w + A_DECAY_LORA + A_AAA_LORA]
    dg = xs[:, 3 * w + A_DECAY_LORA + A_AAA_LORA:A_SHIFT_COLS]

    lw = -A_DECAY_SCALE * jax.nn.sigmoid(w0_ref[...] + _dot(jnp.tanh(dw).astype(BF16), wup_ref[...]))
    a = jax.nn.sigmoid(a0_ref[...] + _dot(da.astype(BF16), aup_ref[...]))
    g = _dot(jax.nn.sigmoid(dg).astype(BF16), gup_ref[...])

    ti = lax.broadcasted_iota(jnp.int32, (tm, tm), 0)
    si = lax.broadcasted_iota(jnp.int32, (tm, tm), 1)
    tri = jnp.where((ti // chunk == si // chunk) & (si <= ti), 1.0, 0.0).astype(BF16)
    lw_hi = lw.astype(BF16)
    lw_lo = (lw - lw_hi.astype(F32)).astype(BF16)
    cs = _dot(tri, lw_hi) + _dot(tri, lw_lo)

    for h in range(A_HEADS):
        sl = slice(h * A_HEAD_DIM, (h + 1) * A_HEAD_DIM)
        r_ref[0, h] = r[:, sl]
        k_ref[0, h] = k[:, sl]
        v_ref[0, h] = v[:, sl]
        lw_ref[0, h] = lw[:, sl]
        cs_ref[0, h] = cs[:, sl]
        a_ref[0, h] = a[:, sl]
        g_ref[0, h] = g[:, sl]


def rwkv_prep(proj, bsz, t_len, mu, w0, w_up, a0, a_up, g_up, chunk, tm=256):
    tpb = t_len // tm
    prev_rows = 2 * SUBLANES
    rb = tm // prev_rows

    def prev_map(b, i):
        return (jnp.maximum((b * tpb + i) * rb - 1, 0), 0)

    head_shape = jax.ShapeDtypeStruct((bsz, A_HEADS, t_len, A_HEAD_DIM), F32)
    head_spec = pl.BlockSpec((1, A_HEADS, tm, A_HEAD_DIM), lambda b, i: (b, 0, i, 0))
    full = lambda shape: pl.BlockSpec(shape, lambda b, i: (0,) * len(shape))
    return pl.pallas_call(
        functools.partial(_rwkv_prep_kernel, chunk=chunk),
        grid=(bsz, tpb),
        in_specs=[
            pl.BlockSpec((tm, PA_PAD), lambda b, i: (b * tpb + i, 0)),
            pl.BlockSpec((prev_rows, PA_PAD), prev_map),
            full((1, A_SHIFT_COLS)),
            full((1, A_WIDTH)),
            full((A_DECAY_LORA, A_WIDTH)),
            full((1, A_WIDTH)),
            full((A_AAA_LORA, A_WIDTH)),
            full((A_GATE_LORA, A_WIDTH)),
        ],
        out_specs=[head_spec] * 7,
        out_shape=[head_shape] * 7,
        compiler_params=_cparams(("parallel", "arbitrary")),
        name="rwkv_prep",
    )(proj, proj, mu, w0, w_up, a0, a_up, g_up)


def _rwkv_scan_kernel(r_ref, k_ref, v_ref, lw_ref, cs_ref, a_ref, g_ref, kk_ref, ka_ref, rk_ref,
                      lnw_ref, lnb_ref, o_ref, s_ref, *, chunk):
    c = chunk
    d = A_HEAD_DIM
    nb = r_ref.shape[0]
    units = [(b, h) for b in range(nb) for h in range(A_HEADS)]
    idx = range(len(units))
    bf = lambda z: z.astype(BF16)

    @pl.when(pl.program_id(1) == 0)
    def _():
        s_ref[...] = jnp.zeros_like(s_ref)

    ti = lax.broadcasted_iota(jnp.int32, (c, c), 0)
    si = lax.broadcasted_iota(jnp.int32, (c, c), 1)
    strict = si < ti
    incl = si <= ti
    n_double = int(np.ceil(np.log2(c)))

    r = [r_ref[b, h] for b, h in units]
    v = [v_ref[b, h] for b, h in units]
    vb = [bf(z) for z in v]
    kmod, at, bt, kt, rt, gend = [], [], [], [], [], []
    for u, (b, h) in enumerate(units):
        k = k_ref[b, h]
        a = a_ref[b, h]
        cs = cs_ref[b, h]
        kk = k * kk_ref[h]
        kk = kk * lax.rsqrt(jnp.maximum(jnp.sum(kk * kk, axis=-1, keepdims=True), 1e-24))
        kmod.append(k * (1.0 + (a - 1.0) * ka_ref[h]))
        gam = jnp.exp(cs)
        ginv = jnp.exp(-cs)
        at.append(-kk * jnp.exp(cs - lw_ref[b, h]))
        bt.append(kk * a * ginv)
        kt.append(kmod[u] * ginv)
        rt.append(r[u] * gam)
        gend.append(gam[c - 1:c, :])

    m_ab, p_rb, m_ak, p_rk = [], [], [], []
    for u in idx:
        lhs = bf(jnp.concatenate([at[u], rt[u]], axis=0))
        fb = _dot_nt(lhs, bf(bt[u]))
        fk = _dot_nt(lhs, bf(kt[u]))
        m_ab.append(jnp.where(strict, fb[:c], 0.0))
        p_rb.append(bf(jnp.where(incl, fb[c:], 0.0)))
        m_ak.append(bf(jnp.where(strict, fk[:c], 0.0)))
        p_rk.append(bf(jnp.where(incl, fk[c:], 0.0)))

    x = [jnp.concatenate([at[u], _dot(m_ak[u], vb[u])], axis=1) for u in idx]
    p = m_ab
    for j in range(n_double):
        pb = [bf(z) for z in p]
        x = [x[u] + _dot(pb[u], bf(x[u])) for u in idx]
        if j + 1 < n_double:
            p = [_dot(pb[u], pb[u]) for u in idx]

    outs = []
    for u, (b, h) in enumerate(units):
        s0 = s_ref[u]
        s0b = bf(s0)
        xh = bf(x[u])
        y = _dot(p_rb[u], xh)
        o = _dot_nt(bf(y[:, :d] + rt[u]), s0b) + y[:, d:] + _dot(p_rk[u], vb[u])

        xb = _dot(bf(x[u].T), bf(bt[u] * gend[u]))
        vk = _dot(bf(v[u].T), bf(kt[u] * gend[u]))
        s_ref[u] = s0 * gend[u] + _dot(s0b, bf(xb[:d])) + xb[d:] + vk

        mean = jnp.mean(o, axis=-1, keepdims=True)
        var = jnp.mean(jnp.square(o - mean), axis=-1, keepdims=True)
        o = (o - mean) * lax.rsqrt(var + A_GN_EPS)
        o = o * lnw_ref[h] + lnb_ref[h]
        bonus = jnp.sum(r[u] * kmod[u] * rk_ref[h], axis=-1, keepdims=True) * v[u]
        outs.append((o + bonus) * g_ref[b, h])

    for b in range(nb):
        o_ref[b] = jnp.concatenate(outs[b * A_HEADS:(b + 1) * A_HEADS], axis=1)


def rwkv_scan(r, k, v, lw, cs, a, g, k_k, k_a, r_k, ln_w, ln_b, chunk, nb=4):
    bsz, _, t_len, _ = r.shape
    assert bsz % nb == 0 and t_len % chunk == 0
    head_spec = pl.BlockSpec((nb, A_HEADS, chunk, A_HEAD_DIM), lambda b, i: (b, 0, i, 0))
    par_spec = pl.BlockSpec((A_HEADS, 1, A_HEAD_DIM), lambda b, i: (0, 0, 0))
    per_head = lambda z: z.reshape(A_HEADS, 1, A_HEAD_DIM)
    return pl.pallas_call(
        functools.partial(_rwkv_scan_kernel, chunk=chunk),
        grid=(bsz // nb, t_len // chunk),
        in_specs=[head_spec] * 7 + [par_spec] * 5,
        out_specs=pl.BlockSpec((nb, chunk, A_WIDTH), lambda b, i: (b, i, 0)),
        out_shape=jax.ShapeDtypeStruct((bsz, t_len, A_WIDTH), F32),
        scratch_shapes=[pltpu.VMEM((nb * A_HEADS, A_HEAD_DIM, A_HEAD_DIM), F32)],
        compiler_params=_cparams(("parallel", "arbitrary")),
        name="rwkv_scan",
    )(r, k, v, lw, cs, a, g, per_head(k_k), per_head(k_a), per_head(r_k), per_head(ln_w),
      per_head(ln_b))


def _sb_attn_kernel(q_ref, k_ref, v_ref, o_ref, *, blk):
    i = pl.program_id(1)
    scale = B_HEAD_DIM ** -0.5
    lane = lax.broadcasted_iota(jnp.int32, (1, LANES), 1)
    head0 = lane < B_HEAD_DIM
    pairs = B_WIDTH // LANES
    zero_q = jnp.zeros((blk, LANES), BF16)
    qh = []
    for p in range(pairs):
        qp = q_ref[:, p * LANES:(p + 1) * LANES] * scale
        qh += [jnp.where(head0, qp, zero_q), jnp.where(head0, zero_q, qp)]

    ju = lax.broadcasted_iota(jnp.int32, (blk, blk + LANES), 0)
    su = lax.broadcasted_iota(jnp.int32, (blk, blk + LANES), 1)
    u = jnp.where((ju > su) | (su >= blk), 1.0, 0.0).astype(BF16)
    u2 = jnp.concatenate([u, u], axis=0)

    def block(kb, cs, accs, diagonal):
        start = pl.multiple_of(kb * blk, blk)
        if diagonal:
            mask = (lax.broadcasted_iota(jnp.int32, (blk, blk), 1)
                    < lax.broadcasted_iota(jnp.int32, (blk, blk), 0))
        heads = range(B_HEADS)
        cols = [pl.ds((h // 2) * LANES, LANES) for h in heads]
        z = [_dot_nt(qh[h], k_ref[pl.ds(start, blk), cols[h]]) for h in heads]
        lbeta, lu = [], []
        for h in heads:
            sp = jnp.maximum(z[h], 0.0) + jnp.log(1.0 + jnp.exp(-jnp.abs(z[h])))
            lbeta.append(z[h] - sp)
            skip = jnp.where(mask, sp, 0.0) if diagonal else sp
            s_hi = skip.astype(BF16)
            s_lo = (skip - s_hi.astype(F32)).astype(BF16)
            lu.append(_dot(jnp.concatenate([s_hi, s_lo], axis=1), u2))
        new_cs, new_accs = [], []
        for h in heads:
            amat = jnp.exp(lbeta[h] - lu[h][:, :blk] + cs[h])
            if diagonal:
                amat = jnp.where(mask, amat, 0.0)
            new_accs.append(accs[h] + _dot(amat.astype(BF16), v_ref[pl.ds(start, blk), cols[h]]))
            new_cs.append(cs[h] - lu[h][:, blk:])
        return new_cs, new_accs

    def cmax_of(cs):
        m = cs[0]
        for z in cs[1:]:
            m = jnp.maximum(m, z)
        return jnp.max(m)

    def cond(carry):
        return jnp.logical_and(carry[0] >= 0, carry[1] > EXP_ZERO_BELOW)

    def body(carry):
        kb = carry[0]
        cs, accs = block(kb, list(carry[2:2 + B_HEADS]), list(carry[2 + B_HEADS:]), diagonal=False)
        return (kb - 1, cmax_of(cs), *cs, *accs)

    zeros = [jnp.zeros((blk, LANES), F32)] * B_HEADS
    cs, accs = block(i, zeros, zeros, diagonal=True)
    out = lax.while_loop(cond, body, (i - 1, cmax_of(cs), *cs, *accs))
    accs = out[2 + B_HEADS:]
    for p in range(pairs):
        o_ref[:, p * LANES:(p + 1) * LANES] = jnp.where(head0, accs[2 * p], accs[2 * p + 1])


def stick_breaking_attention(proj, bsz, t_len, blk=128):
    assert blk == LANES
    nq = t_len // blk
    qc, kc, vc = (PB_OFF // B_WIDTH, PB_OFF // B_WIDTH + 1, PB_OFF // B_WIDTH + 2)
    return pl.pallas_call(
        functools.partial(_sb_attn_kernel, blk=blk),
        grid=(bsz, nq),
        in_specs=[
            pl.BlockSpec((blk, B_WIDTH), lambda b, i: (b * nq + i, qc)),
            pl.BlockSpec((t_len, B_WIDTH), lambda b, i: (b, kc)),
            pl.BlockSpec((t_len, B_WIDTH), lambda b, i: (b, vc)),
        ],
        out_specs=pl.BlockSpec((blk, B_WIDTH), lambda b, i: (b * nq + i, 0)),
        out_shape=jax.ShapeDtypeStruct((bsz * t_len, B_WIDTH), F32),
        compiler_params=_cparams(("parallel", "arbitrary")),
        name="stick_breaking_attention",
    )(proj, proj, proj)


def _diff_attn_kernel(q_ref, k_ref, v_ref, lam_ref, g_ref, o_ref, z_ref, *, blk, lambda_init):
    i = pl.program_id(1)
    lane = lax.broadcasted_iota(jnp.int32, (1, LANES), 1)
    half0 = lane < C_QK_DIM
    zero_q = jnp.zeros((blk, LANES), BF16)
    chains = [(h, s) for h in range(C_HEADS) for s in range(2)]
    qc = []
    for h in range(C_HEADS):
        q = q_ref[:, h * LANES:(h + 1) * LANES]
        qc += [jnp.where(half0, q, zero_q), jnp.where(half0, zero_q, q)]
    nc = len(chains)
    hcols = [pl.ds(h * LANES, LANES) for h, _ in chains]


    def fold(z):
        out = z[:, :LANES]
        for s in range(1, z.shape[1] // LANES):
            out = jnp.maximum(out, z[:, s * LANES:(s + 1) * LANES])
        return out

    def scores(kb, width, diagonal=False):
        rows = pl.ds(pl.multiple_of(kb * blk, blk), width)
        z = [_dot_nt(qc[c], k_ref[rows, hcols[c]]) for c in range(nc)]
        if diagonal:
            mask = (lax.broadcasted_iota(jnp.int32, (blk, blk), 1) // CHUNK
                    <= lax.broadcasted_iota(jnp.int32, (blk, blk), 0) // CHUNK)
            z = [jnp.where(mask, zc, -jnp.inf) for zc in z]
        return z, rows

    pairs = i // 2
    odd = i % 2

    def max_step(kb, width, rm):
        z, _ = scores(kb, width)
        for c in range(nc):
            for s in range(width // blk):
                z_ref[kb + s, c] = z[c][:, s * blk:(s + 1) * blk].astype(BF16)
        return tuple(jnp.maximum(rm[c], fold(z[c])) for c in range(nc))

    rm = (jnp.full((blk, LANES), -jnp.inf, F32),) * nc
    rm = lax.fori_loop(0, pairs, lambda t, rm: max_step(2 * t, 2 * blk, rm), rm)
    rm = lax.fori_loop(0, odd, lambda t, rm: max_step(i - 1, blk, rm), rm)
    zd, rows_d = scores(i, blk, diagonal=True)
    m = [jnp.max(jnp.maximum(rm[c], fold(zd[c])), axis=-1, keepdims=True) for c in range(nc)]

    def accumulate(z, rows, accs):
        ps = [jnp.exp(z[c] - m[c].astype(z[c].dtype)).astype(BF16) for c in range(nc)]
        ones = jnp.ones((z[0].shape[1], LANES), BF16)
        return tuple(accs[c] + _dot(ps[c], jnp.concatenate([v_ref[rows, hcols[c]], ones], axis=1))
                     for c in range(nc))

    def acc_step(kb, width, accs):
        rows = pl.ds(pl.multiple_of(kb * blk, blk), width)
        z = [jnp.concatenate([z_ref[kb + s, c] for s in range(width // blk)], axis=1)
             for c in range(nc)]
        return accumulate(z, rows, accs)

    accs = (jnp.zeros((blk, 2 * LANES), F32),) * nc
    accs = lax.fori_loop(0, pairs, lambda t, accs: acc_step(2 * t, 2 * blk, accs), accs)
    accs = lax.fori_loop(0, odd, lambda t, accs: acc_step(i - 1, blk, accs), accs)
    accs = accumulate(zd, rows_d, accs)

    lv = lam_ref[...]
    lam = (jnp.exp(jnp.sum(lv[0:1] * lv[1:2], axis=-1, keepdims=True))
           - jnp.exp(jnp.sum(lv[2:3] * lv[3:4], axis=-1, keepdims=True)) + lambda_init)
    for h in range(C_HEADS):
        a1, a2 = accs[2 * h], accs[2 * h + 1]
        o = a1[:, :LANES] / a1[:, LANES:] - lam * (a2[:, :LANES] / a2[:, LANES:])
        o = o * lax.rsqrt(jnp.mean(o * o, axis=-1, keepdims=True) + EPS) * g_ref[...]
        o_ref[:, h * LANES:(h + 1) * LANES] = o * (1.0 - lambda_init)


def differential_attention(proj, bsz, t_len, lam_vecs, subln_g, lambda_init, blk=256):
    assert t_len % blk == 0 and C_QK_COLS == C_WIDTH and PC_OFF % C_WIDTH == 0
    nq = t_len // blk
    qc = PC_OFF // C_WIDTH
    return pl.pallas_call(
        functools.partial(_diff_attn_kernel, blk=blk, lambda_init=lambda_init),
        grid=(bsz, nq),
        in_specs=[
            pl.BlockSpec((blk, C_QK_COLS), lambda b, i: (b * nq + i, qc)),
            pl.BlockSpec((t_len, C_QK_COLS), lambda b, i: (b, qc + 1)),
            pl.BlockSpec((t_len, C_WIDTH), lambda b, i: (b, qc + 2)),
            pl.BlockSpec((4, C_QK_DIM), lambda b, i: (0, 0)),
            pl.BlockSpec((1, C_V_DIM), lambda b, i: (0, 0)),
        ],
        out_specs=pl.BlockSpec((blk, C_WIDTH), lambda b, i: (b * nq + i, 0)),
        out_shape=jax.ShapeDtypeStruct((bsz * t_len, C_WIDTH), F32),
        scratch_shapes=[pltpu.VMEM((max(nq - 1, 1), 2 * C_HEADS, blk, blk), BF16)],
        compiler_params=_cparams(("parallel", "arbitrary")),
        name="differential_attention",
    )(proj, proj, proj, lam_vecs, subln_g)


def _merge_kernel(x_ref, oa_ref, ob_ref, oc_ref, ga_ref, gb_ref, gc_ref, wa_ref, wb_ref, wc_ref,
                  wo_ref, o_ref):
    def branch(o_r, g_r, w_r):
        return jax.nn.sigmoid(g_r[...].astype(F32)) * _dot(o_r[...].astype(BF16), w_r[...])

    merged = (branch(oa_ref, ga_ref, wa_ref) + branch(ob_ref, gb_ref, wb_ref)
              + branch(oc_ref, gc_ref, wc_ref))
    o_ref[...] = x_ref[...] + _dot(merged.astype(BF16), wo_ref[...])


def merge_branches(x2, oa, ob, oc, proj, wa, wb, wc, wo, tm=512):
    n = x2.shape[0]
    g0 = PG_OFF // D_MODEL
    row = lambda width: pl.BlockSpec((tm, width), lambda i: (i, 0))
    gate = lambda j: pl.BlockSpec((tm, D_MODEL), lambda i: (i, g0 + j))
    wspec = lambda rows: pl.BlockSpec((rows, D_MODEL), lambda i: (0, 0))
    return pl.pallas_call(
        _merge_kernel,
        grid=(n // tm,),
        in_specs=[row(D_MODEL), row(A_WIDTH), row(B_WIDTH), row(C_WIDTH), gate(0), gate(1), gate(2),
                  wspec(A_WIDTH), wspec(B_WIDTH), wspec(C_WIDTH), wspec(D_MODEL)],
        out_specs=row(D_MODEL),
        out_shape=jax.ShapeDtypeStruct((n, D_MODEL), F32),
        compiler_params=_cparams(("parallel",)),
        name="merge_branches",
    )(x2, oa, ob, oc, proj, proj, proj, wa, wb, wc, wo)


def _ffn_kernel(x_ref, xp_ref, g_ref, wup_ref, cw_ref, cb_ref, wd_ref, gf_ref, o_ref, h_ref, acc_ref,
                *, final_norm, tf):
    halo = FFN_HALO
    nf = D_FF // tf

    def norm(x):
        return x * lax.rsqrt(jnp.mean(x * x, axis=-1, keepdims=True) + EPS)

    g = g_ref[...]
    h_ref[halo:, :] = (norm(x_ref[...]) * g).astype(BF16)
    hp = norm(xp_ref[...]) * g
    hp = jnp.where(pl.program_id(1) == 0, 0.0, hp).astype(BF16)
    h_ref[:halo, :] = jnp.concatenate([jnp.zeros_like(hp)] * (halo // SUBLANES - 1) + [hp], axis=0)

    def up(j):
        hs = h_ref[...]
        return (_dot(hs, wup_ref[:, j * tf:(j + 1) * tf]),
                _dot(hs, wup_ref[:, D_FF + j * tf:D_FF + (j + 1) * tf]))

    def conv(u, col):
        out = cb_ref[:, col:col + tf] + cw_ref[CONV_WIDTH - 1:CONV_WIDTH, col:col + tf] * u
        for back in range(1, CONV_WIDTH):
            tap = CONV_WIDTH - 1 - back
            out = out + cw_ref[tap:tap + 1, col:col + tf] * pltpu.roll(u, back, axis=0)
        return out[halo:]

    def down(j, ug, uv):
        gate = conv(ug, j * tf)
        val = conv(uv, D_FF + j * tf)
        act = (gate * jax.nn.sigmoid(gate) * val).astype(BF16)
        part = _dot(act, wd_ref[j * tf:(j + 1) * tf, :])
        if j == 0:
            acc_ref[...] = part
        else:
            acc_ref[...] += part

    cur = up(0)
    for j in range(nf):
        nxt = up(j + 1) if j + 1 < nf else None
        down(j, *cur)
        cur = nxt

    y = x_ref[...] + acc_ref[...]
    if final_norm:
        y = norm(y) * gf_ref[...]
    o_ref[...] = y


def conv_ffn(x2, bsz, t_len, g, w_up, conv_w, conv_b, w_down, g_final, final_norm, tm=512, tf=256):
    n = x2.shape[0]
    assert t_len % tm == 0 and D_FF % tf == 0
    tpb = t_len // tm
    rb = tm // SUBLANES

    def prev_map(b, i):
        return (jnp.maximum((b * tpb + i) * rb - 1, 0), 0)

    resident = lambda shape: pl.BlockSpec(shape, lambda b, i: (0, 0), pipeline_mode=pl.Buffered(1))
    return pl.pallas_call(
        functools.partial(_ffn_kernel, final_norm=final_norm, tf=tf),
        grid=(bsz, tpb),
        in_specs=[
            pl.BlockSpec((tm, D_MODEL), lambda b, i: (b * tpb + i, 0)),
            pl.BlockSpec((SUBLANES, D_MODEL), prev_map),
            resident((1, D_MODEL)),
            resident((D_MODEL, 2 * D_FF)),
            resident((CONV_WIDTH, 2 * D_FF)),
            resident((1, 2 * D_FF)),
            resident((D_FF, D_MODEL)),
            resident((1, D_MODEL)),
        ],
        out_specs=pl.BlockSpec((tm, D_MODEL), lambda b, i: (b * tpb + i, 0)),
        out_shape=jax.ShapeDtypeStruct((n, D_MODEL), F32),
        scratch_shapes=[
            pltpu.VMEM((tm + FFN_HALO, D_MODEL), BF16),
            pltpu.VMEM((tm, D_MODEL), F32),
        ],
        compiler_params=_cparams(("parallel", "arbitrary")),
        name="conv_ffn",
    )(x2, x2, g, w_up, conv_w, conv_b, w_down, g_final)


def _rope_tables(t_len):
    half = C_ROT_DIMS // 2
    inv_freq = ROPE_THETA ** (-jnp.arange(0, C_ROT_DIMS, 2, dtype=F32) / C_ROT_DIMS)
    ang = jnp.arange(t_len, dtype=F32)[:, None] * inv_freq[None, :]
    cos, sin = jnp.cos(ang), jnp.sin(ang)
    rest = C_QK_DIM - C_ROT_DIMS
    one = jnp.ones((t_len, rest), F32)
    zero = jnp.zeros((t_len, rest), F32)
    zh = jnp.zeros((t_len, half), F32)
    cos_t = jnp.concatenate([cos, cos, one], axis=1)
    sinm_t = jnp.concatenate([-sin, zh, zero], axis=1)
    sinp_t = jnp.concatenate([zh, sin, zero], axis=1)
    dup = lambda z: jnp.concatenate([z, z], axis=1)
    return dup(cos_t), dup(sinm_t), dup(sinp_t)


def _pad_w_in(w):
    pad = jnp.zeros((D_MODEL, PA_PAD - A_SHIFT_COLS), w.dtype)
    return jnp.concatenate([w[:, :A_SHIFT_COLS], pad, w[:, A_SHIFT_COLS:]], axis=1).astype(BF16)


def kernel(x, norm_mix_g, norm_ffn_g, w_in, rwkv_mu, rwkv_w0, rwkv_w_up, rwkv_a0, rwkv_a_up,
           rwkv_g_up, rwkv_k_k, rwkv_k_a, rwkv_r_k, rwkv_ln_w, rwkv_ln_b, diff_lambda, diff_subln_g,
           w_branch_a, w_branch_b, w_branch_c, w_out, ffn_w_up, ffn_conv_w, ffn_conv_b, ffn_w_down,
           norm_final_g):
    bsz, t_len, d_model = x.shape
    assert d_model == D_MODEL and t_len % ROW_TILE == 0, (x.shape, ROW_TILE)
    n = bsz * t_len
    cos_t, sinm_t, sinp_t = _rope_tables(t_len)
    row = lambda z: z.reshape(1, -1)
    x2 = x.reshape(n, D_MODEL)

    for l in range(DEPTH):
        proj = in_projection(x2, row(norm_mix_g[l]), _pad_w_in(w_in[l]), t_len, cos_t, sinm_t, sinp_t)

        r, k, v, lw, cs, a, g = rwkv_prep(
            proj, bsz, t_len, row(rwkv_mu[l]), row(rwkv_w0[l]), rwkv_w_up[l].astype(BF16),
            row(rwkv_a0[l]), rwkv_a_up[l].astype(BF16), rwkv_g_up[l].astype(BF16), RWKV_CHUNK)
        oa = rwkv_scan(r, k, v, lw, cs, a, g, rwkv_k_k[l], rwkv_k_a[l], rwkv_r_k[l].reshape(-1),
                       rwkv_ln_w[l], rwkv_ln_b[l], RWKV_CHUNK).reshape(n, A_WIDTH)

        ob = stick_breaking_attention(proj, bsz, t_len)

        lambda_init = 0.8 - 0.6 * float(np.exp(-0.3 * l))
        oc = differential_attention(proj, bsz, t_len, diff_lambda[l],
                                    row(diff_subln_g[l]), lambda_init)

        x2 = merge_branches(x2, oa, ob, oc, proj, w_branch_a[l].astype(BF16),
                            w_branch_b[l].astype(BF16), w_branch_c[l].astype(BF16),
                            w_out[l].astype(BF16))

        x2 = conv_ffn(x2, bsz, t_len, row(norm_ffn_g[l]), ffn_w_up[l].astype(BF16), ffn_conv_w[l],
                      row(ffn_conv_b[l]), ffn_w_down[l].astype(BF16), row(norm_final_g),
                      final_norm=(l == DEPTH - 1))

    return x2.reshape(bsz, t_len, D_MODEL)
```

```python
import functools

import numpy as np
import jax
import jax.numpy as jnp
from jax import lax
from jax.experimental import pallas as pl
from jax.experimental.pallas import tpu as pltpu

F32 = jnp.float32
BF16 = jnp.bfloat16

D_MODEL = 1024
DEPTH = 4
CHUNK = 64
RWKV_CHUNK = 128
EPS = 1e-6

A_HEADS = 8
A_HEAD_DIM = 64
A_WIDTH = A_HEADS * A_HEAD_DIM
A_DECAY_LORA = 64
A_AAA_LORA = 64
A_GATE_LORA = 128
A_GN_EPS = 64e-5
A_DECAY_SCALE = 0.6065306597126334
A_SHIFT_COLS = 3 * A_WIDTH + A_DECAY_LORA + A_AAA_LORA + A_GATE_LORA

B_HEADS = 8
B_HEAD_DIM = 64
B_WIDTH = B_HEADS * B_HEAD_DIM

C_HEADS = 4
C_QK_DIM = 64
C_V_DIM = 2 * C_QK_DIM
C_QK_COLS = C_HEADS * 2 * C_QK_DIM
C_WIDTH = C_HEADS * C_V_DIM
ROPE_THETA = 500000.0
C_ROT_DIMS = C_QK_DIM // 4

D_FF = 2816
CONV_WIDTH = 3

LANES = 128
SUBLANES = 8

PA_OFF = 0
PA_PAD = 2048
PB_OFF = PA_PAD
PC_OFF = PB_OFF + 3 * B_WIDTH
PG_OFF = PC_OFF + 2 * C_QK_COLS + C_WIDTH
PROJ_COLS = PG_OFF + 3 * D_MODEL

VMEM_LIMIT = 56 * 1024 * 1024
ROW_TILE = 1024
FFN_HALO = 2 * SUBLANES

EXP_ZERO_BELOW = -104.0


def _cparams(sem):
    return pltpu.CompilerParams(dimension_semantics=sem, vmem_limit_bytes=VMEM_LIMIT)


def _dot(a, b):
    return jnp.dot(a, b, preferred_element_type=F32)


def _dot_nt(a, b):
    return lax.dot_general(a, b, (((1,), (1,)), ((), ())), preferred_element_type=F32)


def _inproj_kernel(x_ref, g_ref, w_ref, cos_ref, sinm_ref, sinp_ref, o_ref, *, tn):
    x = x_ref[...]
    ms = jnp.mean(x * x, axis=-1, keepdims=True)
    h = (x * lax.rsqrt(ms + EPS) * g_ref[...]).astype(BF16)
    half = C_ROT_DIMS // 2
    q_scale = C_QK_DIM ** -0.5

    def rot(z):
        return (z * cos_ref[...] + pltpu.roll(z, LANES - half, axis=1) * sinm_ref[...]
                + pltpu.roll(z, half, axis=1) * sinp_ref[...])

    for j in range(PROJ_COLS // tn):
        acc = _dot(h, w_ref[:, j * tn:(j + 1) * tn])
        for s in range(tn // LANES):
            col = j * tn + s * LANES
            blk = acc[:, s * LANES:(s + 1) * LANES]
            if PC_OFF <= col < PC_OFF + C_QK_COLS:
                blk = rot(blk) * q_scale
            elif PC_OFF + C_QK_COLS <= col < PC_OFF + 2 * C_QK_COLS:
                blk = rot(blk)
            o_ref[:, col:col + LANES] = blk.astype(o_ref.dtype)


def in_projection(x2, g, w_pad, t_len, cos_t, sinm_t, sinp_t, tm=512, tn=1024):
    n = x2.shape[0]
    assert t_len % tm == 0 and PROJ_COLS % tn == 0
    tpb = t_len // tm
    resident = lambda shape: pl.BlockSpec(shape, lambda i: (0, 0), pipeline_mode=pl.Buffered(1))
    tab = pl.BlockSpec((tm, LANES), lambda i: (i % tpb, 0))
    return pl.pallas_call(
        functools.partial(_inproj_kernel, tn=tn),
        grid=(n // tm,),
        in_specs=[
            pl.BlockSpec((tm, D_MODEL), lambda i: (i, 0)),
            resident((1, D_MODEL)),
            resident((D_MODEL, PROJ_COLS)),
            tab, tab, tab,
        ],
        out_specs=pl.BlockSpec((tm, PROJ_COLS), lambda i: (i, 0)),
        out_shape=jax.ShapeDtypeStruct((n, PROJ_COLS), BF16),
        compiler_params=_cparams(("parallel",)),
        name="in_projection",
    )(x2, g, w_pad, cos_t, sinm_t, sinp_t)


def _rwkv_prep_kernel(pa_ref, prev_ref, mu_ref, w0_ref, wup_ref, a0_ref, aup_ref, gup_ref,
                      r_ref, k_ref, v_ref, lw_ref, cs_ref, a_ref, g_ref, *, chunk):
    tm = pa_ref.shape[0]
    cur = pa_ref[:, :A_SHIFT_COLS].astype(F32)
    prev_rows = prev_ref.shape[0]
    prev_last = prev_ref[...].astype(F32)[prev_rows - 1:prev_rows, :A_SHIFT_COLS]
    prev_last = jnp.where(pl.program_id(1) == 0, 0.0, prev_last)
    shifted = pltpu.roll(cur, 1, axis=0)
    row = lax.broadcasted_iota(jnp.int32, (tm, 1), 0)
    shifted = jnp.where(row == 0, prev_last, shifted)
    xs = cur + (shifted - cur) * mu_ref[...]

    w = A_WIDTH
    r = xs[:, 0:w]
    k = xs[:, w:2 * w]
    v = xs[:, 2 * w:3 * w]
    dw = xs[:, 3 * w:3 * w + A_DECAY_LORA]
    da = xs[:, 3 * w + A_DECAY_LORA:3 * w + A_DECAY_LORA + A_AAA_LORA]
    dg = xs[:, 3 * w + A_DECAY_LORA + A_AAA_LORA:A_SHIFT_COLS]

    lw = -A_DECAY_SCALE * jax.nn.sigmoid(w0_ref[...] + _dot(jnp.tanh(dw).astype(BF16), wup_ref[...]))
    a = jax.nn.sigmoid(a0_ref[...] + _dot(da.astype(BF16), aup_ref[...]))
    g = _dot(jax.nn.sigmoid(dg).astype(BF16), gup_ref[...])

    ti = lax.broadcasted_iota(jnp.int32, (tm, tm), 0)
    si = lax.broadcasted_iota(jnp.int32, (tm, tm), 1)
    tri = jnp.where((ti // chunk == si // chunk) & (si <= ti), 1.0, 0.0).astype(BF16)
    lw_hi = lw.astype(BF16)
    lw_lo = (lw - lw_hi.astype(F32)).astype(BF16)
    cs = _dot(tri, lw_hi) + _dot(tri, lw_lo)

    for h in range(A_HEADS):
        sl = slice(h * A_HEAD_DIM, (h + 1) * A_HEAD_DIM)
        r_ref[0, h] = r[:, sl]
        k_ref[0, h] = k[:, sl]
        v_ref[0, h] = v[:, sl]
        lw_ref[0, h] = lw[:, sl]
        cs_ref[0, h] = cs[:, sl]
        a_ref[0, h] = a[:, sl]
        g_ref[0, h] = g[:, sl]


def rwkv_prep(proj, bsz, t_len, mu, w0, w_up, a0, a_up, g_up, chunk, tm=256):
    tpb = t_len // tm
    prev_rows = 2 * SUBLANES
    rb = tm // prev_rows

    def prev_map(b, i):
        return (jnp.maximum((b * tpb + i) * rb - 1, 0), 0)

    head_shape = jax.ShapeDtypeStruct((bsz, A_HEADS, t_len, A_HEAD_DIM), F32)
    head_spec = pl.BlockSpec((1, A_HEADS, tm, A_HEAD_DIM), lambda b, i: (b, 0, i, 0))
    full = lambda shape: pl.BlockSpec(shape, lambda b, i: (0,) * len(shape))
    return pl.pallas_call(
        functools.partial(_rwkv_prep_kernel, chunk=chunk),
        grid=(bsz, tpb),
        in_specs=[
            pl.BlockSpec((tm, PA_PAD), lambda b, i: (b * tpb + i, 0)),
            pl.BlockSpec((prev_rows, PA_PAD), prev_map),
            full((1, A_SHIFT_COLS)),
            full((1, A_WIDTH)),
            full((A_DECAY_LORA, A_WIDTH)),
            full((1, A_WIDTH)),
            full((A_AAA_LORA, A_WIDTH)),
            full((A_GATE_LORA, A_WIDTH)),
        ],
        out_specs=[head_spec] * 7,
        out_shape=[head_shape] * 7,
        compiler_params=_cparams(("parallel", "arbitrary")),
        name="rwkv_prep",
    )(proj, proj, mu, w0, w_up, a0, a_up, g_up)


def _rwkv_scan_kernel(r_ref, k_ref, v_ref, lw_ref, cs_ref, a_ref, g_ref, kk_ref, ka_ref, rk_ref,
                      lnw_ref, lnb_ref, o_ref, s_ref, *, chunk):
    c = chunk
    d = A_HEAD_DIM
    nb = r_ref.shape[0]
    units = [(b, h) for b in range(nb) for h in range(A_HEADS)]
    idx = range(len(units))
    bf = lambda z: z.astype(BF16)

    @pl.when(pl.program_id(1) == 0)
    def _():
        s_ref[...] = jnp.zeros_like(s_ref)

    ti = lax.broadcasted_iota(jnp.int32, (c, c), 0)
    si = lax.broadcasted_iota(jnp.int32, (c, c), 1)
    strict = si < ti
    incl = si <= ti
    n_double = int(np.ceil(np.log2(c)))

    r = [r_ref[b, h] for b, h in units]
    v = [v_ref[b, h] for b, h in units]
    vb = [bf(z) for z in v]
    kmod, at, bt, kt, rt, gend = [], [], [], [], [], []
    for u, (b, h) in enumerate(units):
        k = k_ref[b, h]
        a = a_ref[b, h]
        cs = cs_ref[b, h]
        kk = k * kk_ref[h]
        kk = kk * lax.rsqrt(jnp.maximum(jnp.sum(kk * kk, axis=-1, keepdims=True), 1e-24))
        kmod.append(k * (1.0 + (a - 1.0) * ka_ref[h]))
        gam = jnp.exp(cs)
        ginv = jnp.exp(-cs)
        at.append(-kk * jnp.exp(cs - lw_ref[b, h]))
        bt.append(kk * a * ginv)
        kt.append(kmod[u] * ginv)
        rt.append(r[u] * gam)
        gend.append(gam[c - 1:c, :])

    m_ab, p_rb, m_ak, p_rk = [], [], [], []
    for u in idx:
        lhs = bf(jnp.concatenate([at[u], rt[u]], axis=0))
        fb = _dot_nt(lhs, bf(bt[u]))
        fk = _dot_nt(lhs, bf(kt[u]))
        m_ab.append(jnp.where(strict, fb[:c], 0.0))
        p_rb.append(bf(jnp.where(incl, fb[c:], 0.0)))
        m_ak.append(bf(jnp.where(strict, fk[:c], 0.0)))
        p_rk.append(bf(jnp.where(incl, fk[c:], 0.0)))

    x = [jnp.concatenate([at[u], _dot(m_ak[u], vb[u])], axis=1) for u in idx]
    p = m_ab
    for j in range(n_double):
        pb = [bf(z) for z in p]
        x = [x[u] + _dot(pb[u], bf(x[u])) for u in idx]
        if j + 1 < n_double:
            p = [_dot(pb[u], pb[u]) for u in idx]

    outs = []
    for u, (b, h) in enumerate(units):
        s0 = s_ref[u]
        s0b = bf(s0)
        xh = bf(x[u])
        y = _dot(p_rb[u], xh)
        o = _dot_nt(bf(y[:, :d] + rt[u]), s0b) + y[:, d:] + _dot(p_rk[u], vb[u])

        xb = _dot(bf(x[u].T), bf(bt[u] * gend[u]))
        vk = _dot(bf(v[u].T), bf(kt[u] * gend[u]))
        s_ref[u] = s0 * gend[u] + _dot(s0b, bf(xb[:d])) + xb[d:] + vk

        mean = jnp.mean(o, axis=-1, keepdims=True)
        var = jnp.mean(jnp.square(o - mean), axis=-1, keepdims=True)
        o = (o - mean) * lax.rsqrt(var + A_GN_EPS)
        o = o * lnw_ref[h] + lnb_ref[h]
        bonus = jnp.sum(r[u] * kmod[u] * rk_ref[h], axis=-1, keepdims=True) * v[u]
        outs.append((o + bonus) * g_ref[b, h])

    for b in range(nb):
        o_ref[b] = jnp.concatenate(outs[b * A_HEADS:(b + 1) * A_HEADS], axis=1)


def rwkv_scan(r, k, v, lw, cs, a, g, k_k, k_a, r_k, ln_w, ln_b, chunk, nb=4):
    bsz, _, t_len, _ = r.shape
    assert bsz % nb == 0 and t_len % chunk == 0
    head_spec = pl.BlockSpec((nb, A_HEADS, chunk, A_HEAD_DIM), lambda b, i: (b, 0, i, 0))
    par_spec = pl.BlockSpec((A_HEADS, 1, A_HEAD_DIM), lambda b, i: (0, 0, 0))
    per_head = lambda z: z.reshape(A_HEADS, 1, A_HEAD_DIM)
    return pl.pallas_call(
        functools.partial(_rwkv_scan_kernel, chunk=chunk),
        grid=(bsz // nb, t_len // chunk),
        in_specs=[head_spec] * 7 + [par_spec] * 5,
        out_specs=pl.BlockSpec((nb, chunk, A_WIDTH), lambda b, i: (b, i, 0)),
        out_shape=jax.ShapeDtypeStruct((bsz, t_len, A_WIDTH), F32),
        scratch_shapes=[pltpu.VMEM((nb * A_HEADS, A_HEAD_DIM, A_HEAD_DIM), F32)],
        compiler_params=_cparams(("parallel", "arbitrary")),
        name="rwkv_scan",
    )(r, k, v, lw, cs, a, g, per_head(k_k), per_head(k_a), per_head(r_k), per_head(ln_w),
      per_head(ln_b))


def _sb_attn_kernel(q_ref, k_ref, v_ref, o_ref, *, blk):
    i = pl.program_id(1)
    groups = q_ref.shape[0]
    scale = B_HEAD_DIM ** -0.5
    lane = lax.broadcasted_iota(jnp.int32, (1, LANES), 1)
    head0 = lane < B_HEAD_DIM
    pairs = B_WIDTH // LANES
    zero_q = jnp.zeros((blk, LANES), BF16)
    chains = [(g, h) for g in range(groups) for h in range(B_HEADS)]
    nc = len(chains)
    qh = []
    for g in range(groups):
        for p in range(pairs):
            qp = q_ref[g, :, p * LANES:(p + 1) * LANES] * scale
            qh += [jnp.where(head0, qp, zero_q), jnp.where(head0, zero_q, qp)]
    cols = [pl.ds((h // 2) * LANES, LANES) for _, h in chains]

    ju = lax.broadcasted_iota(jnp.int32, (blk, blk + LANES), 0)
    su = lax.broadcasted_iota(jnp.int32, (blk, blk + LANES), 1)
    u = jnp.where((ju > su) | (su >= blk), 1.0, 0.0).astype(BF16)

    def block(kb, cs, accs, diagonal):
        rows = pl.ds(pl.multiple_of(kb * blk, blk), blk)
        if diagonal:
            mask = (lax.broadcasted_iota(jnp.int32, (blk, blk), 1)
                    < lax.broadcasted_iota(jnp.int32, (blk, blk), 0))
        z = [_dot_nt(qh[c], k_ref[g, rows, cols[c]]) for c, (g, _) in enumerate(chains)]
        lbeta, lu = [], []
        for c in range(nc):
            sp = jnp.maximum(z[c], 0.0) + jnp.log(1.0 + jnp.exp(-jnp.abs(z[c])))
            lbeta.append(z[c] - sp)
            skip = jnp.where(mask, sp, 0.0) if diagonal else sp
            lu.append(_dot(skip.astype(BF16), u))
        new_cs, new_accs = [], []
        for c, (g, _) in enumerate(chains):
            amat = jnp.exp(lbeta[c] - lu[c][:, :blk] + cs[c])
            if diagonal:
                amat = jnp.where(mask, amat, 0.0)
            new_accs.append(accs[c] + _dot(amat.astype(BF16), v_ref[g, rows, cols[c]]))
            new_cs.append(cs[c] - lu[c][:, blk:])
        return new_cs, new_accs

    def cmax_of(cs):
        m = cs[0]
        for z in cs[1:]:
            m = jnp.maximum(m, z)
        return jnp.max(m)

    def cond(carry):
        return jnp.logical_and(carry[0] >= 0, carry[1] > EXP_ZERO_BELOW)

    def body(carry):
        kb = carry[0]
        cs, accs = block(kb, list(carry[2:2 + nc]), list(carry[2 + nc:]), diagonal=False)
        return (kb - 1, cmax_of(cs), *cs, *accs)

    zeros = [jnp.zeros((blk, LANES), F32)] * nc
    cs, accs = block(i, zeros, zeros, diagonal=True)
    out = lax.while_loop(cond, body, (i - 1, cmax_of(cs), *cs, *accs))
    accs = out[2 + nc:]
    for g in range(groups):
        for p in range(pairs):
            c0 = g * B_HEADS + 2 * p
            o_ref[g, :, p * LANES:(p + 1) * LANES] = jnp.where(head0, accs[c0], accs[c0 + 1])


def stick_breaking_attention(proj, bsz, t_len, blk=128, groups=2):
    assert blk == LANES and bsz % groups == 0 and t_len % blk == 0
    nq = t_len // blk
    per = bsz // groups
    qc, kc, vc = (PB_OFF // B_WIDTH, PB_OFF // B_WIDTH + 1, PB_OFF // B_WIDTH + 2)
    proj3 = proj.reshape(groups, per * t_len, PROJ_COLS)
    whole = lambda col: pl.BlockSpec((groups, t_len, B_WIDTH), lambda b, i: (0, b, col),
                                     pipeline_mode=pl.Buffered(1))
    out = pl.pallas_call(
        functools.partial(_sb_attn_kernel, blk=blk),
        grid=(per, nq),
        in_specs=[
            pl.BlockSpec((groups, blk, B_WIDTH), lambda b, i: (0, b * nq + i, qc)),
            whole(kc),
            whole(vc),
        ],
        out_specs=pl.BlockSpec((groups, blk, B_WIDTH), lambda b, i: (0, b * nq + i, 0)),
        out_shape=jax.ShapeDtypeStruct((groups, per * t_len, B_WIDTH), F32),
        compiler_params=_cparams(("parallel", "arbitrary")),
        name="stick_breaking_attention",
    )(proj3, proj3, proj3)
    return out.reshape(bsz * t_len, B_WIDTH)


def _diff_attn_kernel(q_ref, k_ref, v_ref, lam_ref, g_ref, o_ref, z_ref, *, blk, lambda_init):
    i = pl.program_id(1)
    lane = lax.broadcasted_iota(jnp.int32, (1, LANES), 1)
    half0 = lane < C_QK_DIM
    zero_q = jnp.zeros((blk, LANES), BF16)
    chains = [(h, s) for h in range(C_HEADS) for s in range(2)]
    qc = []
    for h in range(C_HEADS):
        q = q_ref[:, h * LANES:(h + 1) * LANES]
        qc += [jnp.where(half0, q, zero_q), jnp.where(half0, zero_q, q)]
    nc = len(chains)
    hcols = [pl.ds(h * LANES, LANES) for h, _ in chains]


    def fold(z):
        out = z[:, :LANES]
        for s in range(1, z.shape[1] // LANES):
            out = jnp.maximum(out, z[:, s * LANES:(s + 1) * LANES])
        return out

    def scores(kb, width, diagonal=False):
        rows = pl.ds(pl.multiple_of(kb * blk, blk), width)
        z = [_dot_nt(qc[c], k_ref[rows, hcols[c]]) for c in range(nc)]
        if diagonal:
            mask = (lax.broadcasted_iota(jnp.int32, (blk, blk), 1) // CHUNK
                    <= lax.broadcasted_iota(jnp.int32, (blk, blk), 0) // CHUNK)
            z = [jnp.where(mask, zc, -jnp.inf) for zc in z]
        return z, rows

    pairs = i // 2
    odd = i % 2

    def max_step(kb, width, rm):
        z, _ = scores(kb, width)
        for c in range(nc):
            for s in range(width // blk):
                z_ref[kb + s, c] = z[c][:, s * blk:(s + 1) * blk].astype(BF16)
        return tuple(jnp.maximum(rm[c], fold(z[c])) for c in range(nc))

    rm = (jnp.full((blk, LANES), -jnp.inf, F32),) * nc
    rm = lax.fori_loop(0, pairs, lambda t, rm: max_step(2 * t, 2 * blk, rm), rm)
    rm = lax.fori_loop(0, odd, lambda t, rm: max_step(i - 1, blk, rm), rm)
    zd, rows_d = scores(i, blk, diagonal=True)
    m = [jnp.max(jnp.maximum(rm[c], fold(zd[c])), axis=-1, keepdims=True) for c in range(nc)]

    def accumulate(z, rows, accs):
        ps = [jnp.exp(z[c] - m[c].astype(z[c].dtype)).astype(BF16) for c in range(nc)]
        ones = jnp.ones((z[0].shape[1], LANES), BF16)
        return tuple(accs[c] + _dot(ps[c], jnp.concatenate([v_ref[rows, hcols[c]], ones], axis=1))
                     for c in range(nc))

    def acc_step(kb, width, accs):
        rows = pl.ds(pl.multiple_of(kb * blk, blk), width)
        z = [jnp.concatenate([z_ref[kb + s, c] for s in range(width // blk)], axis=1)
             for c in range(nc)]
        return accumulate(z, rows, accs)

    accs = (jnp.zeros((blk, 2 * LANES), F32),) * nc
    accs = lax.fori_loop(0, pairs, lambda t, accs: acc_step(2 * t, 2 * blk, accs), accs)
    accs = lax.fori_loop(0, odd, lambda t, accs: acc_step(i - 1, blk, accs), accs)
    accs = accumulate(zd, rows_d, accs)

    lv = lam_ref[...]
    lam = (jnp.exp(jnp.sum(lv[0:1] * lv[1:2], axis=-1, keepdims=True))
           - jnp.exp(jnp.sum(lv[2:3] * lv[3:4], axis=-1, keepdims=True)) + lambda_init)
    for h in range(C_HEADS):
        a1, a2 = accs[2 * h], accs[2 * h + 1]
        o = a1[:, :LANES] / a1[:, LANES:] - lam * (a2[:, :LANES] / a2[:, LANES:])
        o = o * lax.rsqrt(jnp.mean(o * o, axis=-1, keepdims=True) + EPS) * g_ref[...]
        o_ref[:, h * LANES:(h + 1) * LANES] = o * (1.0 - lambda_init)


def differential_attention(proj, bsz, t_len, lam_vecs, subln_g, lambda_init, blk=256):
    assert t_len % blk == 0 and C_QK_COLS == C_WIDTH and PC_OFF % C_WIDTH == 0
    nq = t_len // blk
    qc = PC_OFF // C_WIDTH
    return pl.pallas_call(
        functools.partial(_diff_attn_kernel, blk=blk, lambda_init=lambda_init),
        grid=(bsz, nq),
        in_specs=[
            pl.BlockSpec((blk, C_QK_COLS), lambda b, i: (b * nq + i, qc)),
            pl.BlockSpec((t_len, C_QK_COLS), lambda b, i: (b, qc + 1)),
            pl.BlockSpec((t_len, C_WIDTH), lambda b, i: (b, qc + 2)),
            pl.BlockSpec((4, C_QK_DIM), lambda b, i: (0, 0)),
            pl.BlockSpec((1, C_V_DIM), lambda b, i: (0, 0)),
        ],
        out_specs=pl.BlockSpec((blk, C_WIDTH), lambda b, i: (b * nq + i, 0)),
        out_shape=jax.ShapeDtypeStruct((bsz * t_len, C_WIDTH), F32),
        scratch_shapes=[pltpu.VMEM((max(nq - 1, 1), 2 * C_HEADS, blk, blk), BF16)],
        compiler_params=_cparams(("parallel", "arbitrary")),
        name="differential_attention",
    )(proj, proj, proj, lam_vecs, subln_g)


def _merge_kernel(x_ref, oa_ref, ob_ref, oc_ref, ga_ref, gb_ref, gc_ref, wa_ref, wb_ref, wc_ref,
                  wo_ref, o_ref):
    def branch(o_r, g_r, w_r):
        return jax.nn.sigmoid(g_r[...].astype(F32)) * _dot(o_r[...].astype(BF16), w_r[...])

    merged = (branch(oa_ref, ga_ref, wa_ref) + branch(ob_ref, gb_ref, wb_ref)
              + branch(oc_ref, gc_ref, wc_ref))
    o_ref[...] = x_ref[...] + _dot(merged.astype(BF16), wo_ref[...])


def merge_branches(x2, oa, ob, oc, proj, wa, wb, wc, wo, tm=512):
    n = x2.shape[0]
    g0 = PG_OFF // D_MODEL
    row = lambda width: pl.BlockSpec((tm, width), lambda i: (i, 0))
    gate = lambda j: pl.BlockSpec((tm, D_MODEL), lambda i: (i, g0 + j))
    wspec = lambda rows: pl.BlockSpec((rows, D_MODEL), lambda i: (0, 0))
    return pl.pallas_call(
        _merge_kernel,
        grid=(n // tm,),
        in_specs=[row(D_MODEL), row(A_WIDTH), row(B_WIDTH), row(C_WIDTH), gate(0), gate(1), gate(2),
                  wspec(A_WIDTH), wspec(B_WIDTH), wspec(C_WIDTH), wspec(D_MODEL)],
        out_specs=row(D_MODEL),
        out_shape=jax.ShapeDtypeStruct((n, D_MODEL), F32),
        compiler_params=_cparams(("parallel",)),
        name="merge_branches",
    )(x2, oa, ob, oc, proj, proj, proj, wa, wb, wc, wo)


def _ffn_kernel(x_ref, xp_ref, g_ref, wup_ref, cw_ref, cb_ref, wd_ref, gf_ref, o_ref, h_ref, acc_ref,
                *, final_norm, tf):
    halo = FFN_HALO
    nf = D_FF // tf

    def norm(x):
        return x * lax.rsqrt(jnp.mean(x * x, axis=-1, keepdims=True) + EPS)

    g = g_ref[...]
    h_ref[halo:, :] = (norm(x_ref[...]) * g).astype(BF16)
    hp = norm(xp_ref[...]) * g
    hp = jnp.where(pl.program_id(1) == 0, 0.0, hp).astype(BF16)
    h_ref[:halo, :] = jnp.concatenate([jnp.zeros_like(hp)] * (halo // SUBLANES - 1) + [hp], axis=0)

    def up(j):
        hs = h_ref[...]
        return (_dot(hs, wup_ref[:, j * tf:(j + 1) * tf]),
                _dot(hs, wup_ref[:, D_FF + j * tf:D_FF + (j + 1) * tf]))

    def conv(u, col):
        out = cb_ref[:, col:col + tf] + cw_ref[CONV_WIDTH - 1:CONV_WIDTH, col:col + tf] * u
        for back in range(1, CONV_WIDTH):
            tap = CONV_WIDTH - 1 - back
            out = out + cw_ref[tap:tap + 1, col:col + tf] * pltpu.roll(u, back, axis=0)
        return out[halo:]

    def down(j, ug, uv):
        gate = conv(ug, j * tf)
        val = conv(uv, D_FF + j * tf)
        act = (gate * jax.nn.sigmoid(gate) * val).astype(BF16)
        part = _dot(act, wd_ref[j * tf:(j + 1) * tf, :])
        if j == 0:
            acc_ref[...] = part
        else:
            acc_ref[...] += part

    cur = up(0)
    for j in range(nf):
        nxt = up(j + 1) if j + 1 < nf else None
        down(j, *cur)
        cur = nxt

    y = x_ref[...] + acc_ref[...]
    if final_norm:
        y = norm(y) * gf_ref[...]
    o_ref[...] = y


def conv_ffn(x2, bsz, t_len, g, w_up, conv_w, conv_b, w_down, g_final, final_norm, tm=512, tf=256):
    n = x2.shape[0]
    assert t_len % tm == 0 and D_FF % tf == 0
    tpb = t_len // tm
    rb = tm // SUBLANES

    def prev_map(b, i):
        return (jnp.maximum((b * tpb + i) * rb - 1, 0), 0)

    resident = lambda shape: pl.BlockSpec(shape, lambda b, i: (0, 0), pipeline_mode=pl.Buffered(1))
    return pl.pallas_call(
        functools.partial(_ffn_kernel, final_norm=final_norm, tf=tf),
        grid=(bsz, tpb),
        in_specs=[
            pl.BlockSpec((tm, D_MODEL), lambda b, i: (b * tpb + i, 0)),
            pl.BlockSpec((SUBLANES, D_MODEL), prev_map),
            resident((1, D_MODEL)),
            resident((D_MODEL, 2 * D_FF)),
            resident((CONV_WIDTH, 2 * D_FF)),
            resident((1, 2 * D_FF)),
            resident((D_FF, D_MODEL)),
            resident((1, D_MODEL)),
        ],
        out_specs=pl.BlockSpec((tm, D_MODEL), lambda b, i: (b * tpb + i, 0)),
        out_shape=jax.ShapeDtypeStruct((n, D_MODEL), F32),
        scratch_shapes=[
            pltpu.VMEM((tm + FFN_HALO, D_MODEL), BF16),
            pltpu.VMEM((tm, D_MODEL), F32),
        ],
        compiler_params=_cparams(("parallel", "arbitrary")),
        name="conv_ffn",
    )(x2, x2, g, w_up, conv_w, conv_b, w_down, g_final)


def _rope_tables(t_len):
    half = C_ROT_DIMS // 2
    inv_freq = ROPE_THETA ** (-jnp.arange(0, C_ROT_DIMS, 2, dtype=F32) / C_ROT_DIMS)
    ang = jnp.arange(t_len, dtype=F32)[:, None] * inv_freq[None, :]
    cos, sin = jnp.cos(ang), jnp.sin(ang)
    rest = C_QK_DIM - C_ROT_DIMS
    one = jnp.ones((t_len, rest), F32)
    zero = jnp.zeros((t_len, rest), F32)
    zh = jnp.zeros((t_len, half), F32)
    cos_t = jnp.concatenate([cos, cos, one], axis=1)
    sinm_t = jnp.concatenate([-sin, zh, zero], axis=1)
    sinp_t = jnp.concatenate([zh, sin, zero], axis=1)
    dup = lambda z: jnp.concatenate([z, z], axis=1)
    return dup(cos_t), dup(sinm_t), dup(sinp_t)


def _pad_w_in(w):
    pad = jnp.zeros((D_MODEL, PA_PAD - A_SHIFT_COLS), w.dtype)
    return jnp.concatenate([w[:, :A_SHIFT_COLS], pad, w[:, A_SHIFT_COLS:]], axis=1).astype(BF16)


def kernel(x, norm_mix_g, norm_ffn_g, w_in, rwkv_mu, rwkv_w0, rwkv_w_up, rwkv_a0, rwkv_a_up,
           rwkv_g_up, rwkv_k_k, rwkv_k_a, rwkv_r_k, rwkv_ln_w, rwkv_ln_b, diff_lambda, diff_subln_g,
           w_branch_a, w_branch_b, w_branch_c, w_out, ffn_w_up, ffn_conv_w, ffn_conv_b, ffn_w_down,
           norm_final_g):
    bsz, t_len, d_model = x.shape
    assert d_model == D_MODEL and t_len % ROW_TILE == 0, (x.shape, ROW_TILE)
    n = bsz * t_len
    cos_t, sinm_t, sinp_t = _rope_tables(t_len)
    row = lambda z: z.reshape(1, -1)
    x2 = x.reshape(n, D_MODEL)

    for l in range(DEPTH):
        proj = in_projection(x2, row(norm_mix_g[l]), _pad_w_in(w_in[l]), t_len, cos_t, sinm_t, sinp_t)

        r, k, v, lw, cs, a, g = rwkv_prep(
            proj, bsz, t_len, row(rwkv_mu[l]), row(rwkv_w0[l]), rwkv_w_up[l].astype(BF16),
            row(rwkv_a0[l]), rwkv_a_up[l].astype(BF16), rwkv_g_up[l].astype(BF16), RWKV_CHUNK)
        oa = rwkv_scan(r, k, v, lw, cs, a, g, rwkv_k_k[l], rwkv_k_a[l], rwkv_r_k[l].reshape(-1),
                       rwkv_ln_w[l], rwkv_ln_b[l], RWKV_CHUNK).reshape(n, A_WIDTH)

        ob = stick_breaking_attention(proj, bsz, t_len)

        lambda_init = 0.8 - 0.6 * float(np.exp(-0.3 * l))
        oc = differential_attention(proj, bsz, t_len, diff_lambda[l],
                                    row(diff_subln_g[l]), lambda_init)

        x2 = merge_branches(x2, oa, ob, oc, proj, w_branch_a[l].astype(BF16),
                            w_branch_b[l].astype(BF16), w_branch_c[l].astype(BF16),
                            w_out[l].astype(BF16))

        x2 = conv_ffn(x2, bsz, t_len, row(norm_ffn_g[l]), ffn_w_up[l].astype(BF16), ffn_conv_w[l],
                      row(ffn_conv_b[l]), ffn_w_down[l].astype(BF16), row(norm_final_g),
                      final_norm=(l == DEPTH - 1))

    return x2.reshape(bsz, t_len, D_MODEL)
```

```python
import functools

import numpy as np
import jax
import jax.numpy as jnp
from jax import lax
from jax.experimental import pallas as pl
from jax.experimental.pallas import tpu as pltpu

F32 = jnp.float32
BF16 = jnp.bfloat16

D_MODEL = 1024
DEPTH = 4
CHUNK = 64
RWKV_CHUNK = 128
EPS = 1e-6

A_HEADS = 8
A_HEAD_DIM = 64
A_WIDTH = A_HEADS * A_HEAD_DIM
A_DECAY_LORA = 64
A_AAA_LORA = 64
A_GATE_LORA = 128
A_GN_EPS = 64e-5
A_DECAY_SCALE = 0.6065306597126334
A_SHIFT_COLS = 3 * A_WIDTH + A_DECAY_LORA + A_AAA_LORA + A_GATE_LORA

B_HEADS = 8
B_HEAD_DIM = 64
B_WIDTH = B_HEADS * B_HEAD_DIM

C_HEADS = 4
C_QK_DIM = 64
C_V_DIM = 2 * C_QK_DIM
C_QK_COLS = C_HEADS * 2 * C_QK_DIM
C_WIDTH = C_HEADS * C_V_DIM
ROPE_THETA = 500000.0
C_ROT_DIMS = C_QK_DIM // 4

D_FF = 2816
CONV_WIDTH = 3

LANES = 128
SUBLANES = 8

PA_OFF = 0
PA_PAD = 2048
PB_OFF = PA_PAD
PC_OFF = PB_OFF + 3 * B_WIDTH
PG_OFF = PC_OFF + 2 * C_QK_COLS + C_WIDTH
PROJ_COLS = PG_OFF + 3 * D_MODEL

VMEM_LIMIT = 56 * 1024 * 1024
ROW_TILE = 1024
FFN_HALO = 2 * SUBLANES

EXP_ZERO_BELOW = -104.0


def _cparams(sem):
    return pltpu.CompilerParams(dimension_semantics=sem, vmem_limit_bytes=VMEM_LIMIT)


def _dot(a, b):
    return jnp.dot(a, b, preferred_element_type=F32)


def _dot_nt(a, b):
    return lax.dot_general(a, b, (((1,), (1,)), ((), ())), preferred_element_type=F32)


def _inproj_kernel(x_ref, g_ref, w_ref, cos_ref, sinm_ref, sinp_ref, o_ref, *, tn):
    x = x_ref[...]
    ms = jnp.mean(x * x, axis=-1, keepdims=True)
    h = (x * lax.rsqrt(ms + EPS) * g_ref[...]).astype(BF16)
    half = C_ROT_DIMS // 2
    q_scale = C_QK_DIM ** -0.5

    def rot(z):
        return (z * cos_ref[...] + pltpu.roll(z, LANES - half, axis=1) * sinm_ref[...]
                + pltpu.roll(z, half, axis=1) * sinp_ref[...])

    for j in range(PROJ_COLS // tn):
        acc = _dot(h, w_ref[:, j * tn:(j + 1) * tn])
        for s in range(tn // LANES):
            col = j * tn + s * LANES
            blk = acc[:, s * LANES:(s + 1) * LANES]
            if PC_OFF <= col < PC_OFF + C_QK_COLS:
                blk = rot(blk) * q_scale
            elif PC_OFF + C_QK_COLS <= col < PC_OFF + 2 * C_QK_COLS:
                blk = rot(blk)
            o_ref[:, col:col + LANES] = blk.astype(o_ref.dtype)


def in_projection(x2, g, w_pad, t_len, cos_t, sinm_t, sinp_t, tm=512, tn=1024):
    n = x2.shape[0]
    assert t_len % tm == 0 and PROJ_COLS % tn == 0
    tpb = t_len // tm
    resident = lambda shape: pl.BlockSpec(shape, lambda i: (0, 0), pipeline_mode=pl.Buffered(1))
    tab = pl.BlockSpec((tm, LANES), lambda i: (i % tpb, 0))
    return pl.pallas_call(
        functools.partial(_inproj_kernel, tn=tn),
        grid=(n // tm,),
        in_specs=[
            pl.BlockSpec((tm, D_MODEL), lambda i: (i, 0)),
            resident((1, D_MODEL)),
            resident((D_MODEL, PROJ_COLS)),
            tab, tab, tab,
        ],
        out_specs=pl.BlockSpec((tm, PROJ_COLS), lambda i: (i, 0)),
        out_shape=jax.ShapeDtypeStruct((n, PROJ_COLS), BF16),
        compiler_params=_cparams(("parallel",)),
        name="in_projection",
    )(x2, g, w_pad, cos_t, sinm_t, sinp_t)


def _rwkv_prep_kernel(pa_ref, prev_ref, mu_ref, w0_ref, wup_ref, a0_ref, aup_ref, gup_ref,
                      r_ref, k_ref, v_ref, lw_ref, cs_ref, a_ref, g_ref, *, chunk):
    tm = pa_ref.shape[0]
    cur = pa_ref[:, :A_SHIFT_COLS].astype(F32)
    prev_rows = prev_ref.shape[0]
    prev_last = prev_ref[...].astype(F32)[prev_rows - 1:prev_rows, :A_SHIFT_COLS]
    prev_last = jnp.where(pl.program_id(1) == 0, 0.0, prev_last)
    shifted = pltpu.roll(cur, 1, axis=0)
    row = lax.broadcasted_iota(jnp.int32, (tm, 1), 0)
    shifted = jnp.where(row == 0, prev_last, shifted)
    xs = cur + (shifted - cur) * mu_ref[...]

    w = A_WIDTH
    r = xs[:, 0:w]
    k = xs[:, w:2 * w]
    v = xs[:, 2 * w:3 * w]
    dw = xs[:, 3 * w:3 * w + A_DECAY_LORA]
    da = xs[:, 3 * w + A_DECAY_LORA:3 * w + A_DECAY_LORA + A_AAA_LORA]
    dg = xs[:, 3 * w + A_DECAY_LORA + A_AAA_LORA:A_SHIFT_COLS]

    lw = -A_DECAY_SCALE * jax.nn.sigmoid(w0_ref[...] + _dot(jnp.tanh(dw).astype(BF16), wup_ref[...]))
    a = jax.nn.sigmoid(a0_ref[...] + _dot(da.astype(BF16), aup_ref[...]))
    g = _dot(jax.nn.sigmoid(dg).astype(BF16), gup_ref[...])

    ti = lax.broadcasted_iota(jnp.int32, (tm, tm), 0)
    si = lax.broadcasted_iota(jnp.int32, (tm, tm), 1)
    tri = jnp.where((ti // chunk == si // chunk) & (si <= ti), 1.0, 0.0).astype(BF16)
    lw_hi = lw.astype(BF16)
    lw_lo = (lw - lw_hi.astype(F32)).astype(BF16)
    cs = _dot(tri, lw_hi) + _dot(tri, lw_lo)

    for h in range(A_HEADS):
        sl = slice(h * A_HEAD_DIM, (h + 1) * A_HEAD_DIM)
        r_ref[0, h] = r[:, sl].astype(r_ref.dtype)
        k_ref[0, h] = k[:, sl].astype(k_ref.dtype)
        v_ref[0, h] = v[:, sl].astype(v_ref.dtype)
        lw_ref[0, h] = lw[:, sl]
        cs_ref[0, h] = cs[:, sl]
        a_ref[0, h] = a[:, sl].astype(a_ref.dtype)
        g_ref[0, h] = g[:, sl].astype(g_ref.dtype)


def rwkv_prep(proj, bsz, t_len, mu, w0, w_up, a0, a_up, g_up, chunk, tm=256):
    tpb = t_len // tm
    prev_rows = 2 * SUBLANES
    rb = tm // prev_rows

    def prev_map(b, i):
        return (jnp.maximum((b * tpb + i) * rb - 1, 0), 0)

    head_shape = lambda dt: jax.ShapeDtypeStruct((bsz, A_HEADS, t_len, A_HEAD_DIM), dt)
    head_spec = pl.BlockSpec((1, A_HEADS, tm, A_HEAD_DIM), lambda b, i: (b, 0, i, 0))
    full = lambda shape: pl.BlockSpec(shape, lambda b, i: (0,) * len(shape))
    return pl.pallas_call(
        functools.partial(_rwkv_prep_kernel, chunk=chunk),
        grid=(bsz, tpb),
        in_specs=[
            pl.BlockSpec((tm, PA_PAD), lambda b, i: (b * tpb + i, 0)),
            pl.BlockSpec((prev_rows, PA_PAD), prev_map),
            full((1, A_SHIFT_COLS)),
            full((1, A_WIDTH)),
            full((A_DECAY_LORA, A_WIDTH)),
            full((1, A_WIDTH)),
            full((A_AAA_LORA, A_WIDTH)),
            full((A_GATE_LORA, A_WIDTH)),
        ],
        out_specs=[head_spec] * 7,
        out_shape=[head_shape(dt) for dt in (BF16, BF16, BF16, F32, F32, BF16, BF16)],
        compiler_params=_cparams(("parallel", "arbitrary")),
        name="rwkv_prep",
    )(proj, proj, mu, w0, w_up, a0, a_up, g_up)


def _rwkv_scan_kernel(r_ref, k_ref, v_ref, lw_ref, cs_ref, a_ref, g_ref, kk_ref, ka_ref, rk_ref,
                      lnw_ref, lnb_ref, o_ref, s_ref, *, chunk):
    c = chunk
    d = A_HEAD_DIM
    nb = r_ref.shape[0]
    units = [(b, h) for b in range(nb) for h in range(A_HEADS)]
    idx = range(len(units))
    bf = lambda z: z.astype(BF16)

    @pl.when(pl.program_id(1) == 0)
    def _():
        s_ref[...] = jnp.zeros_like(s_ref)

    ti = lax.broadcasted_iota(jnp.int32, (c, c), 0)
    si = lax.broadcasted_iota(jnp.int32, (c, c), 1)
    strict = si < ti
    incl = si <= ti
    n_double = int(np.ceil(np.log2(c)))

    r = [r_ref[b, h].astype(F32) for b, h in units]
    vb = [v_ref[b, h] for b, h in units]
    v = [z.astype(F32) for z in vb]
    kmod, at, bt, kt, rt, gend = [], [], [], [], [], []
    for u, (b, h) in enumerate(units):
        k = k_ref[b, h].astype(F32)
        a = a_ref[b, h].astype(F32)
        cs = cs_ref[b, h]
        kk = k * kk_ref[h]
        kk = kk * lax.rsqrt(jnp.maximum(jnp.sum(kk * kk, axis=-1, keepdims=True), 1e-24))
        kmod.append(k * (1.0 + (a - 1.0) * ka_ref[h]))
        gam = jnp.exp(cs)
        ginv = jnp.exp(-cs)
        at.append(-kk * jnp.exp(cs - lw_ref[b, h]))
        bt.append(kk * a * ginv)
        kt.append(kmod[u] * ginv)
        rt.append(r[u] * gam)
        gend.append(gam[c - 1:c, :])

    m_ab, p_rb, m_ak, p_rk = [], [], [], []
    for u in idx:
        lhs = bf(jnp.concatenate([at[u], rt[u]], axis=0))
        fb = _dot_nt(lhs, bf(bt[u]))
        fk = _dot_nt(lhs, bf(kt[u]))
        m_ab.append(jnp.where(strict, fb[:c], 0.0))
        p_rb.append(bf(jnp.where(incl, fb[c:], 0.0)))
        m_ak.append(bf(jnp.where(strict, fk[:c], 0.0)))
        p_rk.append(bf(jnp.where(incl, fk[c:], 0.0)))

    x = [jnp.concatenate([at[u], _dot(m_ak[u], vb[u])], axis=1) for u in idx]
    p = m_ab
    for j in range(n_double):
        pb = [bf(z) for z in p]
        x = [x[u] + _dot(pb[u], bf(x[u])) for u in idx]
        if j + 1 < n_double:
            p = [_dot(pb[u], pb[u]) for u in idx]

    outs = []
    for u, (b, h) in enumerate(units):
        s0 = s_ref[u]
        s0b = bf(s0)
        xh = bf(x[u])
        y = _dot(p_rb[u], xh)
        o = _dot_nt(bf(y[:, :d] + rt[u]), s0b) + y[:, d:] + _dot(p_rk[u], vb[u])

        xb = _dot(bf(x[u].T), bf(bt[u] * gend[u]))
        vk = _dot(bf(v[u].T), bf(kt[u] * gend[u]))
        s_ref[u] = s0 * gend[u] + _dot(s0b, bf(xb[:d])) + xb[d:] + vk

        mean = jnp.mean(o, axis=-1, keepdims=True)
        var = jnp.mean(jnp.square(o - mean), axis=-1, keepdims=True)
        o = (o - mean) * lax.rsqrt(var + A_GN_EPS)
        o = o * lnw_ref[h] + lnb_ref[h]
        bonus = jnp.sum(r[u] * kmod[u] * rk_ref[h], axis=-1, keepdims=True) * v[u]
        outs.append((o + bonus) * g_ref[b, h].astype(F32))

    for b in range(nb):
        o_ref[b] = jnp.concatenate(outs[b * A_HEADS:(b + 1) * A_HEADS], axis=1)


def rwkv_scan(r, k, v, lw, cs, a, g, k_k, k_a, r_k, ln_w, ln_b, chunk, nb=4):
    bsz, _, t_len, _ = r.shape
    assert bsz % nb == 0 and t_len % chunk == 0
    head_spec = pl.BlockSpec((nb, A_HEADS, chunk, A_HEAD_DIM), lambda b, i: (b, 0, i, 0))
    par_spec = pl.BlockSpec((A_HEADS, 1, A_HEAD_DIM), lambda b, i: (0, 0, 0))
    per_head = lambda z: z.reshape(A_HEADS, 1, A_HEAD_DIM)
    return pl.pallas_call(
        functools.partial(_rwkv_scan_kernel, chunk=chunk),
        grid=(bsz // nb, t_len // chunk),
        in_specs=[head_spec] * 7 + [par_spec] * 5,
        out_specs=pl.BlockSpec((nb, chunk, A_WIDTH), lambda b, i: (b, i, 0)),
        out_shape=jax.ShapeDtypeStruct((bsz, t_len, A_WIDTH), F32),
        scratch_shapes=[pltpu.VMEM((nb * A_HEADS, A_HEAD_DIM, A_HEAD_DIM), F32)],
        compiler_params=_cparams(("parallel", "arbitrary")),
        name="rwkv_scan",
    )(r, k, v, lw, cs, a, g, per_head(k_k), per_head(k_a), per_head(r_k), per_head(ln_w),
      per_head(ln_b))


def _sb_attn_kernel(q_ref, k_ref, v_ref, o_ref, *, blk):
    i = pl.program_id(1)
    groups = q_ref.shape[0]
    scale = B_HEAD_DIM ** -0.5
    lane = lax.broadcasted_iota(jnp.int32, (1, LANES), 1)
    head0 = lane < B_HEAD_DIM
    pairs = B_WIDTH // LANES
    zero_q = jnp.zeros((blk, LANES), BF16)
    chains = [(g, h) for g in range(groups) for h in range(B_HEADS)]
    nc = len(chains)
    qh = []
    for g in range(groups):
        for p in range(pairs):
            qp = q_ref[g, :, p * LANES:(p + 1) * LANES] * scale
            qh += [jnp.where(head0, qp, zero_q), jnp.where(head0, zero_q, qp)]
    cols = [pl.ds((h // 2) * LANES, LANES) for _, h in chains]

    ju = lax.broadcasted_iota(jnp.int32, (blk, blk + LANES), 0)
    su = lax.broadcasted_iota(jnp.int32, (blk, blk + LANES), 1)
    u = jnp.where((ju > su) | (su >= blk), 1.0, 0.0).astype(BF16)

    def block(kb, cs, accs, diagonal):
        rows = pl.ds(pl.multiple_of(kb * blk, blk), blk)
        if diagonal:
            mask = (lax.broadcasted_iota(jnp.int32, (blk, blk), 1)
                    < lax.broadcasted_iota(jnp.int32, (blk, blk), 0))
        z = [_dot_nt(qh[c], k_ref[g, rows, cols[c]]) for c, (g, _) in enumerate(chains)]
        lbeta, lu = [], []
        for c in range(nc):
            sp = jnp.maximum(z[c], 0.0) + jnp.log(1.0 + jnp.exp(-jnp.abs(z[c])))
            lbeta.append(z[c] - sp)
            skip = jnp.where(mask, sp, 0.0) if diagonal else sp
            lu.append(_dot(skip.astype(BF16), u))
        new_cs, new_accs = [], []
        for c, (g, _) in enumerate(chains):
            amat = jnp.exp(lbeta[c] - lu[c][:, :blk] + cs[c])
            if diagonal:
                amat = jnp.where(mask, amat, 0.0)
            new_accs.append(accs[c] + _dot(amat.astype(BF16), v_ref[g, rows, cols[c]]))
            new_cs.append(cs[c] - lu[c][:, blk:])
        return new_cs, new_accs

    def cmax_of(cs):
        m = cs[0]
        for z in cs[1:]:
            m = jnp.maximum(m, z)
        return jnp.max(m)

    def cond(carry):
        return jnp.logical_and(carry[0] >= 0, carry[1] > EXP_ZERO_BELOW)

    def body(carry):
        kb = carry[0]
        cs, accs = block(kb, list(carry[2:2 + nc]), list(carry[2 + nc:]), diagonal=False)
        return (kb - 1, cmax_of(cs), *cs, *accs)

    zeros = [jnp.zeros((blk, LANES), F32)] * nc
    cs, accs = block(i, zeros, zeros, diagonal=True)
    out = lax.while_loop(cond, body, (i - 1, cmax_of(cs), *cs, *accs))
    accs = out[2 + nc:]
    for g in range(groups):
        for p in range(pairs):
            c0 = g * B_HEADS + 2 * p
            o_ref[g, :, p * LANES:(p + 1) * LANES] = jnp.where(head0, accs[c0], accs[c0 + 1])


def stick_breaking_attention(proj, bsz, t_len, blk=128, groups=2):
    assert blk == LANES and bsz % groups == 0 and t_len % blk == 0
    nq = t_len // blk
    per = bsz // groups
    qc, kc, vc = (PB_OFF // B_WIDTH, PB_OFF // B_WIDTH + 1, PB_OFF // B_WIDTH + 2)
    proj3 = proj.reshape(groups, per * t_len, PROJ_COLS)
    whole = lambda col: pl.BlockSpec((groups, t_len, B_WIDTH), lambda b, i: (0, b, col),
                                     pipeline_mode=pl.Buffered(1))
    out = pl.pallas_call(
        functools.partial(_sb_attn_kernel, blk=blk),
        grid=(per, nq),
        in_specs=[
            pl.BlockSpec((groups, blk, B_WIDTH), lambda b, i: (0, b * nq + i, qc)),
            whole(kc),
            whole(vc),
        ],
        out_specs=pl.BlockSpec((groups, blk, B_WIDTH), lambda b, i: (0, b * nq + i, 0)),
        out_shape=jax.ShapeDtypeStruct((groups, per * t_len, B_WIDTH), F32),
        compiler_params=_cparams(("parallel", "arbitrary")),
        name="stick_breaking_attention",
    )(proj3, proj3, proj3)
    return out.reshape(bsz * t_len, B_WIDTH)


def _diff_attn_kernel(q_ref, k_ref, v_ref, lam_ref, g_ref, o_ref, z_ref, *, blk, lambda_init):
    i = pl.program_id(1)
    lane = lax.broadcasted_iota(jnp.int32, (1, LANES), 1)
    half0 = lane < C_QK_DIM
    zero_q = jnp.zeros((blk, LANES), BF16)
    chains = [(h, s) for h in range(C_HEADS) for s in range(2)]
    qc = []
    for h in range(C_HEADS):
        q = q_ref[:, h * LANES:(h + 1) * LANES]
        qc += [jnp.where(half0, q, zero_q), jnp.where(half0, zero_q, q)]
    nc = len(chains)
    hcols = [pl.ds(h * LANES, LANES) for h, _ in chains]


    def fold(z):
        out = z[:, :LANES]
        for s in range(1, z.shape[1] // LANES):
            out = jnp.maximum(out, z[:, s * LANES:(s + 1) * LANES])
        return out

    def scores(kb, width, diagonal=False):
        rows = pl.ds(pl.multiple_of(kb * blk, blk), width)
        z = [_dot_nt(qc[c], k_ref[rows, hcols[c]]) for c in range(nc)]
        if diagonal:
            mask = (lax.broadcasted_iota(jnp.int32, (blk, blk), 1) // CHUNK
                    <= lax.broadcasted_iota(jnp.int32, (blk, blk), 0) // CHUNK)
            z = [jnp.where(mask, zc, -jnp.inf) for zc in z]
        return z, rows

    pairs = i // 2
    odd = i % 2

    def max_step(kb, width, rm):
        z, _ = scores(kb, width)
        for c in range(nc):
            for s in range(width // blk):
                z_ref[kb + s, c] = z[c][:, s * blk:(s + 1) * blk].astype(BF16)
        return tuple(jnp.maximum(rm[c], fold(z[c])) for c in range(nc))

    rm = (jnp.full((blk, LANES), -jnp.inf, F32),) * nc
    rm = lax.fori_loop(0, pairs, lambda t, rm: max_step(2 * t, 2 * blk, rm), rm)
    rm = lax.fori_loop(0, odd, lambda t, rm: max_step(i - 1, blk, rm), rm)
    zd, rows_d = scores(i, blk, diagonal=True)
    m = [jnp.max(jnp.maximum(rm[c], fold(zd[c])), axis=-1, keepdims=True) for c in range(nc)]

    def accumulate(z, rows, accs):
        ps = [jnp.exp(z[c] - m[c].astype(z[c].dtype)).astype(BF16) for c in range(nc)]
        ones = jnp.ones((z[0].shape[1], LANES), BF16)
        return tuple(accs[c] + _dot(ps[c], jnp.concatenate([v_ref[rows, hcols[c]], ones], axis=1))
                     for c in range(nc))

    def acc_step(kb, width, accs):
        rows = pl.ds(pl.multiple_of(kb * blk, blk), width)
        z = [jnp.concatenate([z_ref[kb + s, c] for s in range(width // blk)], axis=1)
             for c in range(nc)]
        return accumulate(z, rows, accs)

    accs = (jnp.zeros((blk, 2 * LANES), F32),) * nc
    accs = lax.fori_loop(0, pairs, lambda t, accs: acc_step(2 * t, 2 * blk, accs), accs)
    accs = lax.fori_loop(0, odd, lambda t, accs: acc_step(i - 1, blk, accs), accs)
    accs = accumulate(zd, rows_d, accs)

    lv = lam_ref[...]
    lam = (jnp.exp(jnp.sum(lv[0:1] * lv[1:2], axis=-1, keepdims=True))
           - jnp.exp(jnp.sum(lv[2:3] * lv[3:4], axis=-1, keepdims=True)) + lambda_init)
    for h in range(C_HEADS):
        a1, a2 = accs[2 * h], accs[2 * h + 1]
        o = a1[:, :LANES] / a1[:, LANES:] - lam * (a2[:, :LANES] / a2[:, LANES:])
        o = o * lax.rsqrt(jnp.mean(o * o, axis=-1, keepdims=True) + EPS) * g_ref[...]
        o_ref[:, h * LANES:(h + 1) * LANES] = o * (1.0 - lambda_init)


def differential_attention(proj, bsz, t_len, lam_vecs, subln_g, lambda_init, blk=256):
    assert t_len % blk == 0 and C_QK_COLS == C_WIDTH and PC_OFF % C_WIDTH == 0
    nq = t_len // blk
    qc = PC_OFF // C_WIDTH
    return pl.pallas_call(
        functools.partial(_diff_attn_kernel, blk=blk, lambda_init=lambda_init),
        grid=(bsz, nq),
        in_specs=[
            pl.BlockSpec((blk, C_QK_COLS), lambda b, i: (b * nq + i, qc)),
            pl.BlockSpec((t_len, C_QK_COLS), lambda b, i: (b, qc + 1)),
            pl.BlockSpec((t_len, C_WIDTH), lambda b, i: (b, qc + 2)),
            pl.BlockSpec((4, C_QK_DIM), lambda b, i: (0, 0)),
            pl.BlockSpec((1, C_V_DIM), lambda b, i: (0, 0)),
        ],
        out_specs=pl.BlockSpec((blk, C_WIDTH), lambda b, i: (b * nq + i, 0)),
        out_shape=jax.ShapeDtypeStruct((bsz * t_len, C_WIDTH), F32),
        scratch_shapes=[pltpu.VMEM((max(nq - 1, 1), 2 * C_HEADS, blk, blk), BF16)],
        compiler_params=_cparams(("parallel", "arbitrary")),
        name="differential_attention",
    )(proj, proj, proj, lam_vecs, subln_g)


def _merge_kernel(x_ref, oa_ref, ob_ref, oc_ref, ga_ref, gb_ref, gc_ref, wa_ref, wb_ref, wc_ref,
                  wo_ref, o_ref):
    def branch(o_r, g_r, w_r):
        return jax.nn.sigmoid(g_r[...].astype(F32)) * _dot(o_r[...].astype(BF16), w_r[...])

    merged = (branch(oa_ref, ga_ref, wa_ref) + branch(ob_ref, gb_ref, wb_ref)
              + branch(oc_ref, gc_ref, wc_ref))
    o_ref[...] = x_ref[...] + _dot(merged.astype(BF16), wo_ref[...])


def merge_branches(x2, oa, ob, oc, proj, wa, wb, wc, wo, tm=512):
    n = x2.shape[0]
    g0 = PG_OFF // D_MODEL
    row = lambda width: pl.BlockSpec((tm, width), lambda i: (i, 0))
    gate = lambda j: pl.BlockSpec((tm, D_MODEL), lambda i: (i, g0 + j))
    wspec = lambda rows: pl.BlockSpec((rows, D_MODEL), lambda i: (0, 0))
    return pl.pallas_call(
        _merge_kernel,
        grid=(n // tm,),
        in_specs=[row(D_MODEL), row(A_WIDTH), row(B_WIDTH), row(C_WIDTH), gate(0), gate(1), gate(2),
                  wspec(A_WIDTH), wspec(B_WIDTH), wspec(C_WIDTH), wspec(D_MODEL)],
        out_specs=row(D_MODEL),
        out_shape=jax.ShapeDtypeStruct((n, D_MODEL), F32),
        compiler_params=_cparams(("parallel",)),
        name="merge_branches",
    )(x2, oa, ob, oc, proj, proj, proj, wa, wb, wc, wo)


def _ffn_kernel(x_ref, xp_ref, g_ref, wup_ref, cw_ref, cb_ref, wd_ref, gf_ref, o_ref, h_ref, acc_ref,
                *, final_norm, tf):
    halo = FFN_HALO
    nf = D_FF // tf

    def norm(x):
        return x * lax.rsqrt(jnp.mean(x * x, axis=-1, keepdims=True) + EPS)

    g = g_ref[...]
    h_ref[halo:, :] = (norm(x_ref[...]) * g).astype(BF16)
    hp = norm(xp_ref[...]) * g
    hp = jnp.where(pl.program_id(1) == 0, 0.0, hp).astype(BF16)
    h_ref[:halo, :] = jnp.concatenate([jnp.zeros_like(hp)] * (halo // SUBLANES - 1) + [hp], axis=0)

    def up(j):
        hs = h_ref[...]
        return (_dot(hs, wup_ref[:, j * tf:(j + 1) * tf]),
                _dot(hs, wup_ref[:, D_FF + j * tf:D_FF + (j + 1) * tf]))

    def conv(u, col):
        out = cb_ref[:, col:col + tf] + cw_ref[CONV_WIDTH - 1:CONV_WIDTH, col:col + tf] * u
        for back in range(1, CONV_WIDTH):
            tap = CONV_WIDTH - 1 - back
            out = out + cw_ref[tap:tap + 1, col:col + tf] * pltpu.roll(u, back, axis=0)
        return out[halo:]

    def down(j, ug, uv):
        gate = conv(ug, j * tf)
        val = conv(uv, D_FF + j * tf)
        act = (gate * jax.nn.sigmoid(gate) * val).astype(BF16)
        part = _dot(act, wd_ref[j * tf:(j + 1) * tf, :])
        if j == 0:
            acc_ref[...] = part
        else:
            acc_ref[...] += part

    cur = up(0)
    for j in range(nf):
        nxt = up(j + 1) if j + 1 < nf else None
        down(j, *cur)
        cur = nxt

    y = x_ref[...] + acc_ref[...]
    if final_norm:
        y = norm(y) * gf_ref[...]
    o_ref[...] = y


def conv_ffn(x2, bsz, t_len, g, w_up, conv_w, conv_b, w_down, g_final, final_norm, tm=512, tf=256):
    n = x2.shape[0]
    assert t_len % tm == 0 and D_FF % tf == 0
    tpb = t_len // tm
    rb = tm // SUBLANES

    def prev_map(b, i):
        return (jnp.maximum((b * tpb + i) * rb - 1, 0), 0)

    resident = lambda shape: pl.BlockSpec(shape, lambda b, i: (0, 0), pipeline_mode=pl.Buffered(1))
    return pl.pallas_call(
        functools.partial(_ffn_kernel, final_norm=final_norm, tf=tf),
        grid=(bsz, tpb),
        in_specs=[
            pl.BlockSpec((tm, D_MODEL), lambda b, i: (b * tpb + i, 0)),
            pl.BlockSpec((SUBLANES, D_MODEL), prev_map),
            resident((1, D_MODEL)),
            resident((D_MODEL, 2 * D_FF)),
            resident((CONV_WIDTH, 2 * D_FF)),
            resident((1, 2 * D_FF)),
            resident((D_FF, D_MODEL)),
            resident((1, D_MODEL)),
        ],
        out_specs=pl.BlockSpec((tm, D_MODEL), lambda b, i: (b * tpb + i, 0)),
        out_shape=jax.ShapeDtypeStruct((n, D_MODEL), F32),
        scratch_shapes=[
            pltpu.VMEM((tm + FFN_HALO, D_MODEL), BF16),
            pltpu.VMEM((tm, D_MODEL), F32),
        ],
        compiler_params=_cparams(("parallel", "arbitrary")),
        name="conv_ffn",
    )(x2, x2, g, w_up, conv_w, conv_b, w_down, g_final)


def _rope_tables(t_len):
    half = C_ROT_DIMS // 2
    inv_freq = ROPE_THETA ** (-jnp.arange(0, C_ROT_DIMS, 2, dtype=F32) / C_ROT_DIMS)
    ang = jnp.arange(t_len, dtype=F32)[:, None] * inv_freq[None, :]
    cos, sin = jnp.cos(ang), jnp.sin(ang)
    rest = C_QK_DIM - C_ROT_DIMS
    one = jnp.ones((t_len, rest), F32)
    zero = jnp.zeros((t_len, rest), F32)
    zh = jnp.zeros((t_len, half), F32)
    cos_t = jnp.concatenate([cos, cos, one], axis=1)
    sinm_t = jnp.concatenate([-sin, zh, zero], axis=1)
    sinp_t = jnp.concatenate([zh, sin, zero], axis=1)
    dup = lambda z: jnp.concatenate([z, z], axis=1)
    return dup(cos_t), dup(sinm_t), dup(sinp_t)


def _pad_w_in(w):
    pad = jnp.zeros((D_MODEL, PA_PAD - A_SHIFT_COLS), w.dtype)
    return jnp.concatenate([w[:, :A_SHIFT_COLS], pad, w[:, A_SHIFT_COLS:]], axis=1).astype(BF16)


def kernel(x, norm_mix_g, norm_ffn_g, w_in, rwkv_mu, rwkv_w0, rwkv_w_up, rwkv_a0, rwkv_a_up,
           rwkv_g_up, rwkv_k_k, rwkv_k_a, rwkv_r_k, rwkv_ln_w, rwkv_ln_b, diff_lambda, diff_subln_g,
           w_branch_a, w_branch_b, w_branch_c, w_out, ffn_w_up, ffn_conv_w, ffn_conv_b, ffn_w_down,
           norm_final_g):
    bsz, t_len, d_model = x.shape
    assert d_model == D_MODEL and t_len % ROW_TILE == 0, (x.shape, ROW_TILE)
    n = bsz * t_len
    cos_t, sinm_t, sinp_t = _rope_tables(t_len)
    row = lambda z: z.reshape(1, -1)
    x2 = x.reshape(n, D_MODEL)

    for l in range(DEPTH):
        proj = in_projection(x2, row(norm_mix_g[l]), _pad_w_in(w_in[l]), t_len, cos_t, sinm_t, sinp_t)

        r, k, v, lw, cs, a, g = rwkv_prep(
            proj, bsz, t_len, row(rwkv_mu[l]), row(rwkv_w0[l]), rwkv_w_up[l].astype(BF16),
            row(rwkv_a0[l]), rwkv_a_up[l].astype(BF16), rwkv_g_up[l].astype(BF16), RWKV_CHUNK)
        oa = rwkv_scan(r, k, v, lw, cs, a, g, rwkv_k_k[l], rwkv_k_a[l], rwkv_r_k[l].reshape(-1),
                       rwkv_ln_w[l], rwkv_ln_b[l], RWKV_CHUNK).reshape(n, A_WIDTH)

        ob = stick_breaking_attention(proj, bsz, t_len)

        lambda_init = 0.8 - 0.6 * float(np.exp(-0.3 * l))
        oc = differential_attention(proj, bsz, t_len, diff_lambda[l],
                                    row(diff_subln_g[l]), lambda_init)

        x2 = merge_branches(x2, oa, ob, oc, proj, w_branch_a[l].astype(BF16),
                            w_branch_b[l].astype(BF16), w_branch_c[l].astype(BF16),
                            w_out[l].astype(BF16))

        x2 = conv_ffn(x2, bsz, t_len, row(norm_ffn_g[l]), ffn_w_up[l].astype(BF16), ffn_conv_w[l],
                      row(ffn_conv_b[l]), ffn_w_down[l].astype(BF16), row(norm_final_g),
                      final_norm=(l == DEPTH - 1))

    return x2.reshape(bsz, t_len, D_MODEL)
```

```python
import functools

import numpy as np
import jax
import jax.numpy as jnp
from jax import lax
from jax.experimental import pallas as pl
from jax.experimental.pallas import tpu as pltpu

F32 = jnp.float32
BF16 = jnp.bfloat16

D_MODEL = 1024
DEPTH = 4
CHUNK = 64
RWKV_CHUNK = 128
EPS = 1e-6

A_HEADS = 8
A_HEAD_DIM = 64
A_WIDTH = A_HEADS * A_HEAD_DIM
A_DECAY_LORA = 64
A_AAA_LORA = 64
A_GATE_LORA = 128
A_GN_EPS = 64e-5
A_DECAY_SCALE = 0.6065306597126334
A_SHIFT_COLS = 3 * A_WIDTH + A_DECAY_LORA + A_AAA_LORA + A_GATE_LORA

B_HEADS = 8
B_HEAD_DIM = 64
B_WIDTH = B_HEADS * B_HEAD_DIM

C_HEADS = 4
C_QK_DIM = 64
C_V_DIM = 2 * C_QK_DIM
C_QK_COLS = C_HEADS * 2 * C_QK_DIM
C_WIDTH = C_HEADS * C_V_DIM
ROPE_THETA = 500000.0
C_ROT_DIMS = C_QK_DIM // 4

D_FF = 2816
CONV_WIDTH = 3

LANES = 128
SUBLANES = 8

PA_OFF = 0
PA_PAD = 2048
PB_OFF = PA_PAD
PC_OFF = PB_OFF + 3 * B_WIDTH
PG_OFF = PC_OFF + 2 * C_QK_COLS + C_WIDTH
PROJ_COLS = PG_OFF + 3 * D_MODEL

VMEM_LIMIT = 56 * 1024 * 1024
ROW_TILE = 1024
FFN_HALO = 2 * SUBLANES

EXP_ZERO_BELOW = -104.0


def _cparams(sem):
    return pltpu.CompilerParams(dimension_semantics=sem, vmem_limit_bytes=VMEM_LIMIT)


def _dot(a, b):
    return jnp.dot(a, b, preferred_element_type=F32)


def _dot_nt(a, b):
    return lax.dot_general(a, b, (((1,), (1,)), ((), ())), preferred_element_type=F32)


def _inproj_kernel(x_ref, g_ref, w_ref, cos_ref, sinm_ref, sinp_ref, o_ref, *, tn):
    x = x_ref[...]
    ms = jnp.mean(x * x, axis=-1, keepdims=True)
    h = (x * lax.rsqrt(ms + EPS) * g_ref[...]).astype(BF16)
    half = C_ROT_DIMS // 2
    q_scale = C_QK_DIM ** -0.5

    def rot(z):
        return (z * cos_ref[...] + pltpu.roll(z, LANES - half, axis=1) * sinm_ref[...]
                + pltpu.roll(z, half, axis=1) * sinp_ref[...])

    for j in range(PROJ_COLS // tn):
        acc = _dot(h, w_ref[:, j * tn:(j + 1) * tn])
        for s in range(tn // LANES):
            col = j * tn + s * LANES
            blk = acc[:, s * LANES:(s + 1) * LANES]
            if PC_OFF <= col < PC_OFF + C_QK_COLS:
                blk = rot(blk) * q_scale
            elif PC_OFF + C_QK_COLS <= col < PC_OFF + 2 * C_QK_COLS:
                blk = rot(blk)
            o_ref[:, col:col + LANES] = blk.astype(o_ref.dtype)


def in_projection(x2, g, w_pad, t_len, cos_t, sinm_t, sinp_t, tm=512, tn=1024):
    n = x2.shape[0]
    assert t_len % tm == 0 and PROJ_COLS % tn == 0
    tpb = t_len // tm
    resident = lambda shape: pl.BlockSpec(shape, lambda i: (0, 0), pipeline_mode=pl.Buffered(1))
    tab = pl.BlockSpec((tm, LANES), lambda i: (i % tpb, 0))
    return pl.pallas_call(
        functools.partial(_inproj_kernel, tn=tn),
        grid=(n // tm,),
        in_specs=[
            pl.BlockSpec((tm, D_MODEL), lambda i: (i, 0)),
            resident((1, D_MODEL)),
            resident((D_MODEL, PROJ_COLS)),
            tab, tab, tab,
        ],
        out_specs=pl.BlockSpec((tm, PROJ_COLS), lambda i: (i, 0)),
        out_shape=jax.ShapeDtypeStruct((n, PROJ_COLS), BF16),
        compiler_params=_cparams(("parallel",)),
        name="in_projection",
    )(x2, g, w_pad, cos_t, sinm_t, sinp_t)


def _rwkv_prep_kernel(pa_ref, prev_ref, mu_ref, w0_ref, wup_ref, a0_ref, aup_ref, gup_ref,
                      r_ref, k_ref, v_ref, lw_ref, cs_ref, a_ref, g_ref, *, chunk):
    tm = pa_ref.shape[0]
    cur = pa_ref[:, :A_SHIFT_COLS].astype(F32)
    prev_rows = prev_ref.shape[0]
    prev_last = prev_ref[...].astype(F32)[prev_rows - 1:prev_rows, :A_SHIFT_COLS]
    prev_last = jnp.where(pl.program_id(1) == 0, 0.0, prev_last)
    shifted = pltpu.roll(cur, 1, axis=0)
    row = lax.broadcasted_iota(jnp.int32, (tm, 1), 0)
    shifted = jnp.where(row == 0, prev_last, shifted)
    xs = cur + (shifted - cur) * mu_ref[...]

    w = A_WIDTH
    r = xs[:, 0:w]
    k = xs[:, w:2 * w]
    v = xs[:, 2 * w:3 * w]
    dw = xs[:, 3 * w:3 * w + A_DECAY_LORA]
    da = xs[:, 3 * w + A_DECAY_LORA:3 * w + A_DECAY_LORA + A_AAA_LORA]
    dg = xs[:, 3 * w + A_DECAY_LORA + A_AAA_LORA:A_SHIFT_COLS]

    lw = -A_DECAY_SCALE * jax.nn.sigmoid(w0_ref[...] + _dot(jnp.tanh(dw).astype(BF16), wup_ref[...]))
    a = jax.nn.sigmoid(a0_ref[...] + _dot(da.astype(BF16), aup_ref[...]))
    g = _dot(jax.nn.sigmoid(dg).astype(BF16), gup_ref[...])

    ti = lax.broadcasted_iota(jnp.int32, (tm, tm), 0)
    si = lax.broadcasted_iota(jnp.int32, (tm, tm), 1)
    tri = jnp.where((ti // chunk == si // chunk) & (si <= ti), 1.0, 0.0).astype(BF16)
    lw_hi = lw.astype(BF16)
    lw_lo = (lw - lw_hi.astype(F32)).astype(BF16)
    cs = _dot(tri, lw_hi) + _dot(tri, lw_lo)

    for h in range(A_HEADS):
        sl = slice(h * A_HEAD_DIM, (h + 1) * A_HEAD_DIM)
        r_ref[0, h] = r[:, sl]
        k_ref[0, h] = k[:, sl]
        v_ref[0, h] = v[:, sl]
        lw_ref[0, h] = lw[:, sl]
        cs_ref[0, h] = cs[:, sl]
        a_ref[0, h] = a[:, sl]
        g_ref[0, h] = g[:, sl]


def rwkv_prep(proj, bsz, t_len, mu, w0, w_up, a0, a_up, g_up, chunk, tm=512):
    tpb = t_len // tm
    prev_rows = 2 * SUBLANES
    rb = tm // prev_rows

    def prev_map(b, i):
        return (jnp.maximum((b * tpb + i) * rb - 1, 0), 0)

    head_shape = jax.ShapeDtypeStruct((bsz, A_HEADS, t_len, A_HEAD_DIM), F32)
    head_spec = pl.BlockSpec((1, A_HEADS, tm, A_HEAD_DIM), lambda b, i: (b, 0, i, 0))
    full = lambda shape: pl.BlockSpec(shape, lambda b, i: (0,) * len(shape))
    return pl.pallas_call(
        functools.partial(_rwkv_prep_kernel, chunk=chunk),
        grid=(bsz, tpb),
        in_specs=[
            pl.BlockSpec((tm, PA_PAD), lambda b, i: (b * tpb + i, 0)),
            pl.BlockSpec((prev_rows, PA_PAD), prev_map),
            full((1, A_SHIFT_COLS)),
            full((1, A_WIDTH)),
            full((A_DECAY_LORA, A_WIDTH)),
            full((1, A_WIDTH)),
            full((A_AAA_LORA, A_WIDTH)),
            full((A_GATE_LORA, A_WIDTH)),
        ],
        out_specs=[head_spec] * 7,
        out_shape=[head_shape] * 7,
        compiler_params=_cparams(("parallel", "arbitrary")),
        name="rwkv_prep",
    )(proj, proj, mu, w0, w_up, a0, a_up, g_up)


def _rwkv_scan_kernel(r_ref, k_ref, v_ref, lw_ref, cs_ref, a_ref, g_ref, kk_ref, ka_ref, rk_ref,
                      lnw_ref, lnb_ref, o_ref, s_ref, *, chunk):
    c = chunk
    d = A_HEAD_DIM
    nb = r_ref.shape[0]
    units = [(b, h) for b in range(nb) for h in range(A_HEADS)]
    idx = range(len(units))
    bf = lambda z: z.astype(BF16)

    @pl.when(pl.program_id(1) == 0)
    def _():
        s_ref[...] = jnp.zeros_like(s_ref)

    ti = lax.broadcasted_iota(jnp.int32, (c, c), 0)
    si = lax.broadcasted_iota(jnp.int32, (c, c), 1)
    strict = si < ti
    incl = si <= ti
    n_double = int(np.ceil(np.log2(c)))

    r = [r_ref[b, h] for b, h in units]
    v = [v_ref[b, h] for b, h in units]
    vb = [bf(z) for z in v]
    kmod, at, bt, kt, rt, gend = [], [], [], [], [], []
    for u, (b, h) in enumerate(units):
        k = k_ref[b, h]
        a = a_ref[b, h]
        cs = cs_ref[b, h]
        kk = k * kk_ref[h]
        kk = kk * lax.rsqrt(jnp.maximum(jnp.sum(kk * kk, axis=-1, keepdims=True), 1e-24))
        kmod.append(k * (1.0 + (a - 1.0) * ka_ref[h]))
        gam = jnp.exp(cs)
        ginv = jnp.exp(-cs)
        at.append(-kk * jnp.exp(cs - lw_ref[b, h]))
        bt.append(kk * a * ginv)
        kt.append(kmod[u] * ginv)
        rt.append(r[u] * gam)
        gend.append(gam[c - 1:c, :])

    m_ab, p_rb, m_ak, p_rk = [], [], [], []
    for u in idx:
        lhs = bf(jnp.concatenate([at[u], rt[u]], axis=0))
        fb = _dot_nt(lhs, bf(bt[u]))
        fk = _dot_nt(lhs, bf(kt[u]))
        m_ab.append(jnp.where(strict, fb[:c], 0.0))
        p_rb.append(bf(jnp.where(incl, fb[c:], 0.0)))
        m_ak.append(bf(jnp.where(strict, fk[:c], 0.0)))
        p_rk.append(bf(jnp.where(incl, fk[c:], 0.0)))

    x = [jnp.concatenate([at[u], _dot(m_ak[u], vb[u])], axis=1) for u in idx]
    p = m_ab
    for j in range(n_double):
        pb = [bf(z) for z in p]
        x = [x[u] + _dot(pb[u], bf(x[u])) for u in idx]
        if j + 1 < n_double:
            p = [_dot(pb[u], pb[u]) for u in idx]

    outs = []
    for u, (b, h) in enumerate(units):
        s0 = s_ref[u]
        s0b = bf(s0)
        xh = bf(x[u])
        y = _dot(p_rb[u], xh)
        o = _dot_nt(bf(y[:, :d] + rt[u]), s0b) + y[:, d:] + _dot(p_rk[u], vb[u])

        xb = _dot(bf(x[u].T), bf(bt[u] * gend[u]))
        vk = _dot(bf(v[u].T), bf(kt[u] * gend[u]))
        s_ref[u] = s0 * gend[u] + _dot(s0b, bf(xb[:d])) + xb[d:] + vk

        mean = jnp.mean(o, axis=-1, keepdims=True)
        var = jnp.mean(jnp.square(o - mean), axis=-1, keepdims=True)
        o = (o - mean) * lax.rsqrt(var + A_GN_EPS)
        o = o * lnw_ref[h] + lnb_ref[h]
        bonus = jnp.sum(r[u] * kmod[u] * rk_ref[h], axis=-1, keepdims=True) * v[u]
        outs.append((o + bonus) * g_ref[b, h])

    for b in range(nb):
        o_ref[b] = jnp.concatenate(outs[b * A_HEADS:(b + 1) * A_HEADS], axis=1)


def rwkv_scan(r, k, v, lw, cs, a, g, k_k, k_a, r_k, ln_w, ln_b, chunk, nb=4):
    bsz, _, t_len, _ = r.shape
    assert bsz % nb == 0 and t_len % chunk == 0
    head_spec = pl.BlockSpec((nb, A_HEADS, chunk, A_HEAD_DIM), lambda b, i: (b, 0, i, 0))
    par_spec = pl.BlockSpec((A_HEADS, 1, A_HEAD_DIM), lambda b, i: (0, 0, 0))
    per_head = lambda z: z.reshape(A_HEADS, 1, A_HEAD_DIM)
    return pl.pallas_call(
        functools.partial(_rwkv_scan_kernel, chunk=chunk),
        grid=(bsz // nb, t_len // chunk),
        in_specs=[head_spec] * 7 + [par_spec] * 5,
        out_specs=pl.BlockSpec((nb, chunk, A_WIDTH), lambda b, i: (b, i, 0)),
        out_shape=jax.ShapeDtypeStruct((bsz, t_len, A_WIDTH), F32),
        scratch_shapes=[pltpu.VMEM((nb * A_HEADS, A_HEAD_DIM, A_HEAD_DIM), F32)],
        compiler_params=_cparams(("parallel", "arbitrary")),
        name="rwkv_scan",
    )(r, k, v, lw, cs, a, g, per_head(k_k), per_head(k_a), per_head(r_k), per_head(ln_w),
      per_head(ln_b))


def _sb_attn_kernel(q_ref, k_ref, v_ref, o_ref, *, blk):
    i = pl.program_id(1)
    groups = q_ref.shape[0]
    scale = B_HEAD_DIM ** -0.5
    lane = lax.broadcasted_iota(jnp.int32, (1, LANES), 1)
    head0 = lane < B_HEAD_DIM
    pairs = B_WIDTH // LANES
    zero_q = jnp.zeros((blk, LANES), BF16)
    chains = [(g, h) for g in range(groups) for h in range(B_HEADS)]
    nc = len(chains)
    qh = []
    for g in range(groups):
        for p in range(pairs):
            qp = q_ref[g, :, p * LANES:(p + 1) * LANES] * scale
            qh += [jnp.where(head0, qp, zero_q), jnp.where(head0, zero_q, qp)]
    cols = [pl.ds((h // 2) * LANES, LANES) for _, h in chains]

    ju = lax.broadcasted_iota(jnp.int32, (blk, blk + LANES), 0)
    su = lax.broadcasted_iota(jnp.int32, (blk, blk + LANES), 1)
    u = jnp.where((ju > su) | (su >= blk), 1.0, 0.0).astype(BF16)

    def block(kb, cs, accs, diagonal):
        rows = pl.ds(pl.multiple_of(kb * blk, blk), blk)
        if diagonal:
            mask = (lax.broadcasted_iota(jnp.int32, (blk, blk), 1)
                    < lax.broadcasted_iota(jnp.int32, (blk, blk), 0))
        z = [_dot_nt(qh[c], k_ref[g, rows, cols[c]]) for c, (g, _) in enumerate(chains)]
        lbeta, lu = [], []
        for c in range(nc):
            sp = jnp.maximum(z[c], 0.0) + jnp.log(1.0 + jnp.exp(-jnp.abs(z[c])))
            lbeta.append(z[c] - sp)
            skip = jnp.where(mask, sp, 0.0) if diagonal else sp
            lu.append(_dot(skip.astype(BF16), u))
        new_cs, new_accs = [], []
        for c, (g, _) in enumerate(chains):
            amat = jnp.exp(lbeta[c] - lu[c][:, :blk] + cs[c])
            if diagonal:
                amat = jnp.where(mask, amat, 0.0)
            new_accs.append(accs[c] + _dot(amat.astype(BF16), v_ref[g, rows, cols[c]]))
            new_cs.append(cs[c] - lu[c][:, blk:])
        return new_cs, new_accs

    def cmax_of(cs):
        m = cs[0]
        for z in cs[1:]:
            m = jnp.maximum(m, z)
        return jnp.max(m)

    def cond(carry):
        return jnp.logical_and(carry[0] >= 0, carry[1] > EXP_ZERO_BELOW)

    def body(carry):
        kb = carry[0]
        cs, accs = block(kb, list(carry[2:2 + nc]), list(carry[2 + nc:]), diagonal=False)
        return (kb - 1, cmax_of(cs), *cs, *accs)

    zeros = [jnp.zeros((blk, LANES), F32)] * nc
    cs, accs = block(i, zeros, zeros, diagonal=True)
    out = lax.while_loop(cond, body, (i - 1, cmax_of(cs), *cs, *accs))
    accs = out[2 + nc:]
    for g in range(groups):
        for p in range(pairs):
            c0 = g * B_HEADS + 2 * p
            o_ref[g, :, p * LANES:(p + 1) * LANES] = jnp.where(head0, accs[c0], accs[c0 + 1])


def stick_breaking_attention(proj, bsz, t_len, blk=128, groups=2):
    assert blk == LANES and bsz % groups == 0 and t_len % blk == 0
    nq = t_len // blk
    per = bsz // groups
    qc, kc, vc = (PB_OFF // B_WIDTH, PB_OFF // B_WIDTH + 1, PB_OFF // B_WIDTH + 2)
    proj3 = proj.reshape(groups, per * t_len, PROJ_COLS)
    whole = lambda col: pl.BlockSpec((groups, t_len, B_WIDTH), lambda b, i: (0, b, col),
                                     pipeline_mode=pl.Buffered(1))
    out = pl.pallas_call(
        functools.partial(_sb_attn_kernel, blk=blk),
        grid=(per, nq),
        in_specs=[
            pl.BlockSpec((groups, blk, B_WIDTH), lambda b, i: (0, b * nq + i, qc)),
            whole(kc),
            whole(vc),
        ],
        out_specs=pl.BlockSpec((groups, blk, B_WIDTH), lambda b, i: (0, b * nq + i, 0)),
        out_shape=jax.ShapeDtypeStruct((groups, per * t_len, B_WIDTH), F32),
        compiler_params=_cparams(("parallel", "arbitrary")),
        name="stick_breaking_attention",
    )(proj3, proj3, proj3)
    return out.reshape(bsz * t_len, B_WIDTH)


def _diff_attn_kernel(q_ref, k_ref, v_ref, lam_ref, g_ref, o_ref, z_ref, *, blk, lambda_init):
    i = pl.program_id(1)
    lane = lax.broadcasted_iota(jnp.int32, (1, LANES), 1)
    half0 = lane < C_QK_DIM
    zero_q = jnp.zeros((blk, LANES), BF16)
    chains = [(h, s) for h in range(C_HEADS) for s in range(2)]
    qc = []
    for h in range(C_HEADS):
        q = q_ref[:, h * LANES:(h + 1) * LANES]
        qc += [jnp.where(half0, q, zero_q), jnp.where(half0, zero_q, q)]
    nc = len(chains)
    hcols = [pl.ds(h * LANES, LANES) for h, _ in chains]


    def fold(z):
        out = z[:, :LANES]
        for s in range(1, z.shape[1] // LANES):
            out = jnp.maximum(out, z[:, s * LANES:(s + 1) * LANES])
        return out

    def scores(kb, width, diagonal=False):
        rows = pl.ds(pl.multiple_of(kb * blk, blk), width)
        z = [_dot_nt(qc[c], k_ref[rows, hcols[c]]) for c in range(nc)]
        if diagonal:
            mask = (lax.broadcasted_iota(jnp.int32, (blk, blk), 1) // CHUNK
                    <= lax.broadcasted_iota(jnp.int32, (blk, blk), 0) // CHUNK)
            z = [jnp.where(mask, zc, -jnp.inf) for zc in z]
        return z, rows

    pairs = i // 2
    odd = i % 2

    def max_step(kb, width, rm):
        z, _ = scores(kb, width)
        for c in range(nc):
            for s in range(width // blk):
                z_ref[kb + s, c] = z[c][:, s * blk:(s + 1) * blk].astype(BF16)
        return tuple(jnp.maximum(rm[c], fold(z[c])) for c in range(nc))

    rm = (jnp.full((blk, LANES), -jnp.inf, F32),) * nc
    rm = lax.fori_loop(0, pairs, lambda t, rm: max_step(2 * t, 2 * blk, rm), rm)
    rm = lax.fori_loop(0, odd, lambda t, rm: max_step(i - 1, blk, rm), rm)
    zd, rows_d = scores(i, blk, diagonal=True)
    m = [jnp.max(jnp.maximum(rm[c], fold(zd[c])), axis=-1, keepdims=True) for c in range(nc)]

    def accumulate(z, rows, accs):
        ps = [jnp.exp(z[c] - m[c].astype(z[c].dtype)).astype(BF16) for c in range(nc)]
        ones = jnp.ones((z[0].shape[1], LANES), BF16)
        return tuple(accs[c] + _dot(ps[c], jnp.concatenate([v_ref[rows, hcols[c]], ones], axis=1))
                     for c in range(nc))

    def acc_step(kb, width, accs):
        rows = pl.ds(pl.multiple_of(kb * blk, blk), width)
        z = [jnp.concatenate([z_ref[kb + s, c] for s in range(width // blk)], axis=1)
             for c in range(nc)]
        return accumulate(z, rows, accs)

    accs = (jnp.zeros((blk, 2 * LANES), F32),) * nc
    accs = lax.fori_loop(0, pairs, lambda t, accs: acc_step(2 * t, 2 * blk, accs), accs)
    accs = lax.fori_loop(0, odd, lambda t, accs: acc_step(i - 1, blk, accs), accs)
    accs = accumulate(zd, rows_d, accs)

    lv = lam_ref[...]
    lam = (jnp.exp(jnp.sum(lv[0:1] * lv[1:2], axis=-1, keepdims=True))
           - jnp.exp(jnp.sum(lv[2:3] * lv[3:4], axis=-1, keepdims=True)) + lambda_init)
    for h in range(C_HEADS):
        a1, a2 = accs[2 * h], accs[2 * h + 1]
        o = a1[:, :LANES] / a1[:, LANES:] - lam * (a2[:, :LANES] / a2[:, LANES:])
        o = o * lax.rsqrt(jnp.mean(o * o, axis=-1, keepdims=True) + EPS) * g_ref[...]
        o_ref[:, h * LANES:(h + 1) * LANES] = o * (1.0 - lambda_init)


def differential_attention(proj, bsz, t_len, lam_vecs, subln_g, lambda_init, blk=256):
    assert t_len % blk == 0 and C_QK_COLS == C_WIDTH and PC_OFF % C_WIDTH == 0
    nq = t_len // blk
    qc = PC_OFF // C_WIDTH
    return pl.pallas_call(
        functools.partial(_diff_attn_kernel, blk=blk, lambda_init=lambda_init),
        grid=(bsz, nq),
        in_specs=[
            pl.BlockSpec((blk, C_QK_COLS), lambda b, i: (b * nq + i, qc)),
            pl.BlockSpec((t_len, C_QK_COLS), lambda b, i: (b, qc + 1)),
            pl.BlockSpec((t_len, C_WIDTH), lambda b, i: (b, qc + 2)),
            pl.BlockSpec((4, C_QK_DIM), lambda b, i: (0, 0)),
            pl.BlockSpec((1, C_V_DIM), lambda b, i: (0, 0)),
        ],
        out_specs=pl.BlockSpec((blk, C_WIDTH), lambda b, i: (b * nq + i, 0)),
        out_shape=jax.ShapeDtypeStruct((bsz * t_len, C_WIDTH), F32),
        scratch_shapes=[pltpu.VMEM((max(nq - 1, 1), 2 * C_HEADS, blk, blk), BF16)],
        compiler_params=_cparams(("parallel", "arbitrary")),
        name="differential_attention",
    )(proj, proj, proj, lam_vecs, subln_g)


def _merge_kernel(x_ref, oa_ref, ob_ref, oc_ref, ga_ref, gb_ref, gc_ref, wa_ref, wb_ref, wc_ref,
                  wo_ref, o_ref):
    def branch(o_r, g_r, w_r):
        return jax.nn.sigmoid(g_r[...].astype(F32)) * _dot(o_r[...].astype(BF16), w_r[...])

    merged = (branch(oa_ref, ga_ref, wa_ref) + branch(ob_ref, gb_ref, wb_ref)
              + branch(oc_ref, gc_ref, wc_ref))
    o_ref[...] = x_ref[...] + _dot(merged.astype(BF16), wo_ref[...])


def merge_branches(x2, oa, ob, oc, proj, wa, wb, wc, wo, tm=512):
    n = x2.shape[0]
    g0 = PG_OFF // D_MODEL
    row = lambda width: pl.BlockSpec((tm, width), lambda i: (i, 0))
    gate = lambda j: pl.BlockSpec((tm, D_MODEL), lambda i: (i, g0 + j))
    wspec = lambda rows: pl.BlockSpec((rows, D_MODEL), lambda i: (0, 0))
    return pl.pallas_call(
        _merge_kernel,
        grid=(n // tm,),
        in_specs=[row(D_MODEL), row(A_WIDTH), row(B_WIDTH), row(C_WIDTH), gate(0), gate(1), gate(2),
                  wspec(A_WIDTH), wspec(B_WIDTH), wspec(C_WIDTH), wspec(D_MODEL)],
        out_specs=row(D_MODEL),
        out_shape=jax.ShapeDtypeStruct((n, D_MODEL), F32),
        compiler_params=_cparams(("parallel",)),
        name="merge_branches",
    )(x2, oa, ob, oc, proj, proj, proj, wa, wb, wc, wo)


def _ffn_kernel(x_ref, xp_ref, g_ref, wup_ref, cw_ref, cb_ref, wd_ref, gf_ref, o_ref, h_ref,
                *, final_norm, tf):
    halo = FFN_HALO
    nf = D_FF // tf

    def norm(x):
        return x * lax.rsqrt(jnp.mean(x * x, axis=-1, keepdims=True) + EPS)

    g = g_ref[...]
    h_ref[halo:, :] = (norm(x_ref[...]) * g).astype(BF16)
    hp = norm(xp_ref[...]) * g
    hp = jnp.where(pl.program_id(1) == 0, 0.0, hp).astype(BF16)
    h_ref[:halo, :] = jnp.concatenate([jnp.zeros_like(hp)] * (halo // SUBLANES - 1) + [hp], axis=0)

    def up(j):
        hs = h_ref[...]
        return (_dot(hs, wup_ref[:, j * tf:(j + 1) * tf]),
                _dot(hs, wup_ref[:, D_FF + j * tf:D_FF + (j + 1) * tf]))

    def conv(u, col):
        out = cb_ref[:, col:col + tf] + cw_ref[CONV_WIDTH - 1:CONV_WIDTH, col:col + tf] * u
        for back in range(1, CONV_WIDTH):
            tap = CONV_WIDTH - 1 - back
            out = out + cw_ref[tap:tap + 1, col:col + tf] * pltpu.roll(u, back, axis=0)
        return out[halo:]

    def down(j, ug, uv):
        gate = conv(ug, j * tf)
        val = conv(uv, D_FF + j * tf)
        act = (gate * jax.nn.sigmoid(gate) * val).astype(BF16)
        return _dot(act, wd_ref[j * tf:(j + 1) * tf, :])

    cur = up(0)
    acc = None
    for j in range(nf):
        nxt = up(j + 1) if j + 1 < nf else None
        part = down(j, *cur)
        acc = part if acc is None else acc + part
        cur = nxt

    y = x_ref[...] + acc
    if final_norm:
        y = norm(y) * gf_ref[...]
    o_ref[...] = y


def conv_ffn(x2, bsz, t_len, g, w_up, conv_w, conv_b, w_down, g_final, final_norm, tm=512, tf=256):
    n = x2.shape[0]
    assert t_len % tm == 0 and D_FF % tf == 0
    tpb = t_len // tm
    rb = tm // SUBLANES

    def prev_map(b, i):
        return (jnp.maximum((b * tpb + i) * rb - 1, 0), 0)

    resident = lambda shape: pl.BlockSpec(shape, lambda b, i: (0, 0), pipeline_mode=pl.Buffered(1))
    return pl.pallas_call(
        functools.partial(_ffn_kernel, final_norm=final_norm, tf=tf),
        grid=(bsz, tpb),
        in_specs=[
            pl.BlockSpec((tm, D_MODEL), lambda b, i: (b * tpb + i, 0)),
            pl.BlockSpec((SUBLANES, D_MODEL), prev_map),
            resident((1, D_MODEL)),
            resident((D_MODEL, 2 * D_FF)),
            resident((CONV_WIDTH, 2 * D_FF)),
            resident((1, 2 * D_FF)),
            resident((D_FF, D_MODEL)),
            resident((1, D_MODEL)),
        ],
        out_specs=pl.BlockSpec((tm, D_MODEL), lambda b, i: (b * tpb + i, 0)),
        out_shape=jax.ShapeDtypeStruct((n, D_MODEL), F32),
        scratch_shapes=[pltpu.VMEM((tm + FFN_HALO, D_MODEL), BF16)],
        compiler_params=_cparams(("parallel", "arbitrary")),
        name="conv_ffn",
    )(x2, x2, g, w_up, conv_w, conv_b, w_down, g_final)


def _rope_tables(t_len):
    half = C_ROT_DIMS // 2
    inv_freq = ROPE_THETA ** (-jnp.arange(0, C_ROT_DIMS, 2, dtype=F32) / C_ROT_DIMS)
    ang = jnp.arange(t_len, dtype=F32)[:, None] * inv_freq[None, :]
    cos, sin = jnp.cos(ang), jnp.sin(ang)
    rest = C_QK_DIM - C_ROT_DIMS
    one = jnp.ones((t_len, rest), F32)
    zero = jnp.zeros((t_len, rest), F32)
    zh = jnp.zeros((t_len, half), F32)
    cos_t = jnp.concatenate([cos, cos, one], axis=1)
    sinm_t = jnp.concatenate([-sin, zh, zero], axis=1)
    sinp_t = jnp.concatenate([zh, sin, zero], axis=1)
    dup = lambda z: jnp.concatenate([z, z], axis=1)
    return dup(cos_t), dup(sinm_t), dup(sinp_t)


def _pad_w_in(w):
    pad = jnp.zeros((D_MODEL, PA_PAD - A_SHIFT_COLS), w.dtype)
    return jnp.concatenate([w[:, :A_SHIFT_COLS], pad, w[:, A_SHIFT_COLS:]], axis=1).astype(BF16)


def kernel(x, norm_mix_g, norm_ffn_g, w_in, rwkv_mu, rwkv_w0, rwkv_w_up, rwkv_a0, rwkv_a_up,
           rwkv_g_up, rwkv_k_k, rwkv_k_a, rwkv_r_k, rwkv_ln_w, rwkv_ln_b, diff_lambda, diff_subln_g,
           w_branch_a, w_branch_b, w_branch_c, w_out, ffn_w_up, ffn_conv_w, ffn_conv_b, ffn_w_down,
           norm_final_g):
    bsz, t_len, d_model = x.shape
    assert d_model == D_MODEL and t_len % ROW_TILE == 0, (x.shape, ROW_TILE)
    n = bsz * t_len
    cos_t, sinm_t, sinp_t = _rope_tables(t_len)
    row = lambda z: z.reshape(1, -1)
    x2 = x.reshape(n, D_MODEL)

    for l in range(DEPTH):
        proj = in_projection(x2, row(norm_mix_g[l]), _pad_w_in(w_in[l]), t_len, cos_t, sinm_t, sinp_t)

        r, k, v, lw, cs, a, g = rwkv_prep(
            proj, bsz, t_len, row(rwkv_mu[l]), row(rwkv_w0[l]), rwkv_w_up[l].astype(BF16),
            row(rwkv_a0[l]), rwkv_a_up[l].astype(BF16), rwkv_g_up[l].astype(BF16), RWKV_CHUNK)
        oa = rwkv_scan(r, k, v, lw, cs, a, g, rwkv_k_k[l], rwkv_k_a[l], rwkv_r_k[l].reshape(-1),
                       rwkv_ln_w[l], rwkv_ln_b[l], RWKV_CHUNK).reshape(n, A_WIDTH)

        ob = stick_breaking_attention(proj, bsz, t_len)

        lambda_init = 0.8 - 0.6 * float(np.exp(-0.3 * l))
        oc = differential_attention(proj, bsz, t_len, diff_lambda[l],
                                    row(diff_subln_g[l]), lambda_init)

        x2 = merge_branches(x2, oa, ob, oc, proj, w_branch_a[l].astype(BF16),
                            w_branch_b[l].astype(BF16), w_branch_c[l].astype(BF16),
                            w_out[l].astype(BF16))

        x2 = conv_ffn(x2, bsz, t_len, row(norm_ffn_g[l]), ffn_w_up[l].astype(BF16), ffn_conv_w[l],
                      row(ffn_conv_b[l]), ffn_w_down[l].astype(BF16), row(norm_final_g),
                      final_norm=(l == DEPTH - 1))

    return x2.reshape(bsz, t_len, D_MODEL)
```
